```python
import math
import jax, jax.numpy as jnp
from jax import lax
import numpy as np

D_MODEL = 1024
BATCH = 2
SEQ = 16384
DEPTH = 4

CTX_LEN = 256
GRID_W = 64
HEAD_DIM = 64
ROPE_THETA = 10000.0
EPS = 1e-6
NEG_INF = -1e30

NA_HEADS = 8
NA_WIN_H = 8
NA_WIN_W = 16
DIFF_HEADS = 4
DIFF_QK_DIM = 64
DIFF_V_DIM = 128
DENSE_BLOCK = 128
SWA_Q_HEADS = 16
SWA_KV_HEADS = 4
SWA_WINDOW = 128
SWA_BLOCK = 128
PEER_HEADS = 8
PEER_NKEYS = 128
PEER_EXPERTS = PEER_NKEYS * PEER_NKEYS
PEER_QDIM = 256
PEER_TOPK = 16
PEER_CHUNK = 128

EVEN_IN = 3 * NA_HEADS * HEAD_DIM + DIFF_HEADS * (4 * DIFF_QK_DIM + DIFF_V_DIM)
EVEN_OUT = NA_HEADS * HEAD_DIM + DIFF_HEADS * DIFF_V_DIM
ODD_IN = (SWA_Q_HEADS + 2 * SWA_KV_HEADS) * HEAD_DIM
ODD_OUT = SWA_Q_HEADS * HEAD_DIM
N_EVEN = (DEPTH + 1) // 2
N_ODD = DEPTH // 2

kernel_name = "hybrid_natten_diff_swa_peer_dit"


def rmsnorm(x, g):
    xf = x.astype(jnp.float32)
    y = xf * lax.rsqrt(jnp.mean(xf * xf, axis=-1, keepdims=True) + EPS)
    return (y * g.astype(jnp.float32)).astype(x.dtype)


def split_mod(m, lead):
    m = m.reshape(lead + (6, D_MODEL))
    return tuple(m[..., k, :] for k in range(6))


def rope_tables(n_tokens):
    t = jnp.arange(n_tokens)
    row = (t // GRID_W).astype(jnp.float32)
    col = (t % GRID_W).astype(jnp.float32)
    half = HEAD_DIM // 2
    inv = ROPE_THETA ** (-jnp.arange(0, half, 2, dtype=jnp.float32) / half)
    ar = row[:, None] * inv
    ac = col[:, None] * inv
    ang = jnp.concatenate([ar, ar, ac, ac], axis=-1)
    return jnp.cos(ang), jnp.sin(ang)


def apply_rope2d(x, cos, sin):
    q = HEAD_DIM // 4
    rot = jnp.concatenate([-x[..., q:2 * q], x[..., :q], -x[..., 3 * q:], x[..., 2 * q:3 * q]], axis=-1)
    shp = (x.shape[1],) + (1,) * (x.ndim - 3) + (HEAD_DIM,)
    return (x * cos.reshape(shp) + rot * sin.reshape(shp)).astype(x.dtype)


def plain_ctx_attention(qc, kc, vc):
    B, C, H, d = qc.shape
    s = jnp.einsum('bqhd,bkhd->bhqk', qc, kc).astype(jnp.float32) * (d ** -0.5)
    p = jax.nn.softmax(s, axis=-1).astype(vc.dtype)
    return jnp.einsum('bhqk,bkhd->bqhd', p, vc).reshape(B, C, H * d)


def neighborhood_attention(q, k, v, qc, kc, vc, rpb, need_ctx):
    B, S, H, d = q.shape
    rows = S // GRID_W
    kh = min(NA_WIN_H, rows)
    kw = NA_WIN_W
    scale = d ** -0.5
    qg = q.reshape(B, rows, GRID_W, H, d)
    kg = k.reshape(B, rows, GRID_W, H, d)
    vg = v.reshape(B, rows, GRID_W, H, d)
    cols = jnp.arange(GRID_W)
    col_start = jnp.clip(cols - kw // 2, 0, GRID_W - kw)
    col_idx = col_start[:, None] + jnp.arange(kw)
    dc = col_idx - cols[:, None] + (NA_WIN_W - 1)
    rpb_c = rpb[:, :, dc]

    def one_row(args):
        r, q_row = args
        rs = jnp.clip(r - kh // 2, 0, rows - kh)
        k_rows = lax.dynamic_slice_in_dim(kg, rs, kh, axis=1)
        v_rows = lax.dynamic_slice_in_dim(vg, rs, kh, axis=1)
        k_nb = k_rows[:, :, col_idx]
        v_nb = v_rows[:, :, col_idx]
        dr = rs + jnp.arange(kh) - r + (NA_WIN_H - 1)
        bias = rpb_c[:, dr].transpose(0, 2, 1, 3)
        s_nb = jnp.einsum('bqhd,brqwhd->bhqrw', q_row, k_nb).astype(jnp.float32) * scale
        s_nb = (s_nb + bias[None].astype(jnp.float32)).reshape(B, H, GRID_W, kh * kw)
        s_ctx = jnp.einsum('bqhd,bchd->bhqc', q_row, kc).astype(jnp.float32) * scale
        p = jax.nn.softmax(jnp.concatenate([s_nb, s_ctx], axis=-1), axis=-1).astype(v.dtype)
        p_nb = p[..., :kh * kw].reshape(B, H, GRID_W, kh, kw)
        p_ctx = p[..., kh * kw:]
        return (jnp.einsum('bhqrw,brqwhd->bqhd', p_nb, v_nb)
                + jnp.einsum('bhqc,bchd->bqhd', p_ctx, vc))

    out = lax.map(one_row, (jnp.arange(rows), qg.transpose(1, 0, 2, 3, 4)))
    out = out.transpose(1, 0, 2, 3, 4).reshape(B, S, H * d)
    out_c = plain_ctx_attention(qc, kc, vc) if need_ctx else None
    return out, out_c


def diff_attention(q, k, v, qc, kc, vc, lam_p, subln_g, lam_init, need_ctx):
    B, S, H, _, dq = q.shape
    dv = v.shape[-1]
    scale = dq ** -0.5
    lp = lam_p.astype(jnp.float32)
    lam = jnp.exp(jnp.sum(lp[0] * lp[1])) - jnp.exp(jnp.sum(lp[2] * lp[3])) + lam_init

    def attend(qb, kk, vv):
        s = jnp.einsum('bqhmd,bkhmd->bmhqk', qb, kk).astype(jnp.float32) * scale
        p = jax.nn.softmax(s, axis=-1)
        pd = (p[:, 0] - lam * p[:, 1]).astype(vv.dtype)
        o = jnp.einsum('bhqk,bkhd->bqhd', pd, vv)
        return (rmsnorm(o, subln_g) * (1.0 - lam_init)).astype(vv.dtype)

    k_all = jnp.concatenate([kc, k], axis=1)
    v_all = jnp.concatenate([vc, v], axis=1)
    nb = S // DENSE_BLOCK
    qblocks = q.reshape(B, nb, DENSE_BLOCK, H, 2, dq).transpose(1, 0, 2, 3, 4, 5)
    out = lax.map(lambda qb: attend(qb, k_all, v_all), qblocks)
    out = out.transpose(1, 0, 2, 3, 4).reshape(B, S, H * dv)
    out_c = attend(qc, kc, vc).reshape(B, qc.shape[1], H * dv) if need_ctx else None
    return out, out_c


def window_gqa(q, k, v, qc, kc, vc, sink, need_ctx):
    B, S, Hq, d = q.shape
    Hkv = k.shape[2]
    G = Hq // Hkv
    scale = d ** -0.5
    blk = SWA_BLOCK
    nb = S // blk
    pad = ((0, 0), (blk, blk), (0, 0), (0, 0))
    kp = jnp.pad(k, pad)
    vp = jnp.pad(v, pad)
    qi = jnp.arange(blk)
    kj = jnp.arange(3 * blk)
    band = jnp.abs(kj[None, :] - blk - qi[:, None]) <= SWA_WINDOW
    sink_f = sink.astype(jnp.float32).reshape(Hkv, G)

    def sink_attend(qg, s_parts, v_parts):
        n = qg.shape[1]
        s_sink = jnp.broadcast_to(sink_f[None, :, :, None, None], (B, Hkv, G, n, 1))
        p = jax.nn.softmax(jnp.concatenate([s_sink] + s_parts, axis=-1), axis=-1)
        o = 0.0
        off = 1
        for s_i, v_i in zip(s_parts, v_parts):
            w = s_i.shape[-1]
            o = o + jnp.einsum('bngqj,bjnd->bqngd', p[..., off:off + w].astype(v_i.dtype), v_i)
            off += w
        return o.reshape(B, n, Hq * d).astype(v_parts[0].dtype)

    def one_block(args):
        bi, qb = args
        start = bi * blk
        kb = lax.dynamic_slice_in_dim(kp, start, 3 * blk, axis=1)
        vb = lax.dynamic_slice_in_dim(vp, start, 3 * blk, axis=1)
        kpos = start - blk + kj
        valid = band & ((kpos >= 0) & (kpos < S))[None, :]
        qg = qb.reshape(B, blk, Hkv, G, d)
        s_loc = jnp.einsum('bqngd,bjnd->bngqj', qg, kb).astype(jnp.float32) * scale
        s_loc = jnp.where(valid, s_loc, NEG_INF)
        s_ctx = jnp.einsum('bqngd,bjnd->bngqj', qg, kc).astype(jnp.float32) * scale
        return sink_attend(qg, [s_loc, s_ctx], [vb, vc])

    qblocks = q.reshape(B, nb, blk, Hq, d).transpose(1, 0, 2, 3, 4)
    out = lax.map(one_block, (jnp.arange(nb), qblocks))
    out = out.transpose(1, 0, 2, 3).reshape(B, S, Hq * d)
    out_c = None
    if need_ctx:
        qcg = qc.reshape(B, qc.shape[1], Hkv, G, d)
        s_c = jnp.einsum('bqngd,bjnd->bngqj', qcg, kc).astype(jnp.float32) * scale
        out_c = sink_attend(qcg, [s_c], [vc])
    return out, out_c


def even_mixer(hx, hc, w_in, w_out, rpb, lam_p, subln_g, cos, sin, lam_init, need_ctx):
    B, S, _ = hx.shape
    a = NA_HEADS * HEAD_DIM
    bqk = DIFF_HEADS * 2 * DIFF_QK_DIM

    def split(p, n):
        qa = p[..., :a].reshape(B, n, NA_HEADS, HEAD_DIM)
        ka = p[..., a:2 * a].reshape(B, n, NA_HEADS, HEAD_DIM)
        va = p[..., 2 * a:3 * a].reshape(B, n, NA_HEADS, HEAD_DIM)
        o = 3 * a
        qb = p[..., o:o + bqk].reshape(B, n, DIFF_HEADS, 2, DIFF_QK_DIM)
        kb = p[..., o + bqk:o + 2 * bqk].reshape(B, n, DIFF_HEADS, 2, DIFF_QK_DIM)
        vb = p[..., o + 2 * bqk:].reshape(B, n, DIFF_HEADS, DIFF_V_DIM)
        return qa, ka, va, qb, kb, vb

    qa, ka, va, qb, kb, vb = split(hx @ w_in, S)
    qac, kac, vac, qbc, kbc, vbc = split(hc @ w_in, hc.shape[1])
    qb = apply_rope2d(qb, cos, sin)
    kb = apply_rope2d(kb, cos, sin)
    oa, oac = neighborhood_attention(qa, ka, va, qac, kac, vac, rpb, need_ctx)
    ob, obc = diff_attention(qb, kb, vb, qbc, kbc, vbc, lam_p, subln_g, lam_init, need_ctx)
    y = jnp.concatenate([oa, ob], axis=-1) @ w_out
    yc = jnp.concatenate([oac, obc], axis=-1) @ w_out if need_ctx else None
    return y, yc


def odd_mixer(hx, hc, w_in, w_out, sink, cos, sin, need_ctx):
    B, S, _ = hx.shape
    nq = SWA_Q_HEADS * HEAD_DIM
    nkv = SWA_KV_HEADS * HEAD_DIM

    def split(p, n):
        return (p[..., :nq].reshape(B, n, SWA_Q_HEADS, HEAD_DIM),
                p[..., nq:nq + nkv].reshape(B, n, SWA_KV_HEADS, HEAD_DIM),
                p[..., nq + nkv:].reshape(B, n, SWA_KV_HEADS, HEAD_DIM))

    q, k, v = split(hx @ w_in, S)
    qc, kc, vc = split(hc @ w_in, hc.shape[1])
    q = apply_rope2d(q, cos, sin)
    k = apply_rope2d(k, cos, sin)
    o, oc = window_gqa(q, k, v, qc, kc, vc, sink, need_ctx)
    y = o @ w_out
    yc = oc @ w_out if need_ctx else None
    return y, yc


def peer(h, w_q, sub_keys, u, v):
    T, D = h.shape
    q = (h @ w_q).reshape(T, PEER_HEADS, 2, PEER_QDIM // 2)
    s = jnp.einsum('thpd,hpnd->thpn', q, sub_keys).astype(jnp.float32)
    sv, si = lax.top_k(s, PEER_TOPK)
    cand = (sv[:, :, 0, :, None] + sv[:, :, 1, None, :]).reshape(T, PEER_HEADS, PEER_TOPK * PEER_TOPK)
    fv, fi = lax.top_k(cand, PEER_TOPK)
    i1 = jnp.take_along_axis(si[:, :, 0], fi // PEER_TOPK, axis=-1)
    i2 = jnp.take_along_axis(si[:, :, 1], fi % PEER_TOPK, axis=-1)
    idx = (i1 * PEER_NKEYS + i2).reshape(T, PEER_HEADS * PEER_TOPK)
    g = jax.nn.softmax(fv, axis=-1).reshape(T, PEER_HEADS * PEER_TOPK).astype(h.dtype)
    nc = T // PEER_CHUNK

    def chunk(args):
        hc, ic, gc = args
        ue = u[ic]
        ve = v[ic]
        act = jax.nn.gelu(jnp.einsum('td,ted->te', hc, ue), approximate=False)
        return jnp.einsum('te,ted->td', gc * act, ve)

    out = lax.map(chunk, (h.reshape(nc, PEER_CHUNK, D),
                          idx.reshape(nc, PEER_CHUNK, -1),
                          g.reshape(nc, PEER_CHUNK, -1)))
    return out.reshape(T, D)


def setup_inputs(seed: int = 0) -> dict:
    key = jax.random.key(seed)
    ks = jax.random.split(key, 24)
    D = D_MODEL

    def nrm(k, shape, s):
        return jax.random.normal(k, shape, jnp.float32) * s

    return {
        "x": nrm(ks[0], (BATCH, SEQ, D), 1.0),
        "c": nrm(ks[1], (BATCH, D), 1.0),
        "ctx": nrm(ks[2], (BATCH, CTX_LEN, D), 1.0),
        "c_ctx": nrm(ks[3], (D,), 1.0),
        "w_mod": nrm(ks[4], (DEPTH, D, 6 * D), 0.5 * D ** -0.5),
        "b_mod": nrm(ks[5], (DEPTH, 6 * D), 0.02),
        "norm1_g": 1.0 + nrm(ks[6], (DEPTH, D), 0.05),
        "norm2_g": 1.0 + nrm(ks[7], (DEPTH, D), 0.05),
        "w_in_even": nrm(ks[8], (N_EVEN, D, EVEN_IN), D ** -0.5),
        "w_out_even": nrm(ks[9], (N_EVEN, EVEN_OUT, D), EVEN_OUT ** -0.5),
        "na_rpb": nrm(ks[10], (N_EVEN, NA_HEADS, 2 * NA_WIN_H - 1, 2 * NA_WIN_W - 1), 0.1),
        "diff_lambda": nrm(ks[11], (N_EVEN, 4, DIFF_QK_DIM), 0.1),
        "diff_subln_g": 1.0 + nrm(ks[12], (N_EVEN, DIFF_V_DIM), 0.05),
        "w_in_odd": nrm(ks[13], (N_ODD, D, ODD_IN), D ** -0.5),
        "w_out_odd": nrm(ks[14], (N_ODD, ODD_OUT, D), ODD_OUT ** -0.5),
        "swa_sink": nrm(ks[15], (N_ODD, SWA_Q_HEADS), 0.5),
        "peer_wq": nrm(ks[16], (DEPTH, D, PEER_HEADS * PEER_QDIM), D ** -0.5),
        "peer_keys": nrm(ks[17], (DEPTH, PEER_HEADS, 2, PEER_NKEYS, PEER_QDIM // 2), (PEER_QDIM // 2) ** -0.5),
        "peer_u": nrm(ks[18], (DEPTH, PEER_EXPERTS, D), D ** -0.5),
        "peer_v": nrm(ks[19], (DEPTH, PEER_EXPERTS, D), 0.25),
        "final_g": 1.0 + nrm(ks[20], (D,), 0.05),
    }


def reference(x, c, ctx, c_ctx, w_mod, b_mod, norm1_g, norm2_g, w_in_even, w_out_even,
              na_rpb, diff_lambda, diff_subln_g, w_in_odd, w_out_odd, swa_sink,
              peer_wq, peer_keys, peer_u, peer_v, final_g):
    B, S, D = x.shape
    C = ctx.shape[1]
    cos, sin = rope_tables(S)
    sc = jax.nn.silu(c)
    scc = jax.nn.silu(c_ctx)
    for i in range(DEPTH):
        need_ctx = i < DEPTH - 1
        sh1, sc1, g1, sh2, sc2, g2 = split_mod(sc @ w_mod[i] + b_mod[i], (B, 1))
        ch1, cc1, cg1, ch2, cc2, cg2 = split_mod(scc @ w_mod[i] + b_mod[i], (1, 1))
        hx = rmsnorm(x, norm1_g[i]) * (1.0 + sc1) + sh1
        hc = rmsnorm(ctx, norm1_g[i]) * (1.0 + cc1) + ch1
        if i % 2 == 0:
            j = i // 2
            lam_init = 0.8 - 0.6 * math.exp(-0.3 * i)
            y, yc = even_mixer(hx, hc, w_in_even[j], w_out_even[j], na_rpb[j], diff_lambda[j],
                               diff_subln_g[j], cos, sin, lam_init, need_ctx)
        else:
            j = i // 2
            y, yc = odd_mixer(hx, hc, w_in_odd[j], w_out_odd[j], swa_sink[j], cos, sin, need_ctx)
        x = x + g1 * y
        hx = rmsnorm(x, norm2_g[i]) * (1.0 + sc2) + sh2
        if need_ctx:
            ctx = ctx + cg1 * yc
            hc = rmsnorm(ctx, norm2_g[i]) * (1.0 + cc2) + ch2
            tokens = jnp.concatenate([hx.reshape(B * S, D), hc.reshape(B * C, D)], axis=0)
            f = peer(tokens, peer_wq[i], peer_keys[i], peer_u[i], peer_v[i])
            x = x + g2 * f[:B * S].reshape(B, S, D)
            ctx = ctx + cg2 * f[B * S:].reshape(B, C, D)
        else:
            f = peer(hx.reshape(B * S, D), peer_wq[i], peer_keys[i], peer_u[i], peer_v[i])
            x = x + g2 * f.reshape(B, S, D)
    return rmsnorm(x, final_g)
```

```python
import functools
import math

import jax
import jax.numpy as jnp
from jax import lax
from jax.experimental import pallas as pl
from jax.experimental.pallas import tpu as pltpu

F32 = jnp.float32
BF16 = jnp.bfloat16

D_MODEL = 1024
CTX_LEN = 256
GRID_W = 64
HEAD_DIM = 64
ROPE_THETA = 10000.0
EPS = 1e-6
NEG_INF = -1e30

NA_HEADS = 8
NA_WIN_H = 8
NA_WIN_W = 16
DIFF_HEADS = 4
SWA_Q_HEADS = 16
SWA_KV_HEADS = 4
SWA_WINDOW = 128
PEER_HEADS = 8
PEER_NKEYS = 128
PEER_TOPK = 16

LANES = 128
TOK_TILE = 512
ATT_BLOCK = 256
NA_ROWS_PER_BLOCK = ATT_BLOCK // GRID_W
NA_WIN_BLOCKS = 3
DIFF_TQ = 512
DIFF_TK = 512
PEER_TE = 512
NOT_TOPK = 999.0
MOD_ROWS = 8
VMEM_LIMIT = 56 * 1024 * 1024


def _cparams(sem):
    return pltpu.CompilerParams(dimension_semantics=sem, vmem_limit_bytes=VMEM_LIMIT)


def _dot(a, b):
    return jnp.dot(a, b, preferred_element_type=F32)


def _dot_nt(a, b):
    return lax.dot_general(a, b, (((1,), (1,)), ((), ())), preferred_element_type=F32)


def _rep_lanes(v, n):
    return v if n == 1 else jnp.concatenate([v] * n, axis=1)


def _mod_row(mod_ref, i, n_batch, tiles_per_batch):
    r = jnp.where(i == 0, n_batch, jnp.maximum(i - 1, 0) // tiles_per_batch)
    return mod_ref[pl.ds(r, 1), :]


def _rmsnorm_rows(x, g):
    ms = jnp.mean(x * x, axis=-1, keepdims=True)
    return x * lax.rsqrt(ms + EPS) * g


def _mod_kernel(cv_ref, w_ref, b_ref, o_ref):
    cv = cv_ref[...]
    s = cv * (1.0 / (1.0 + jnp.exp(-cv)))
    o_ref[0] = jnp.dot(s, w_ref[0], preferred_element_type=F32,
                       precision=lax.Precision.HIGHEST) + b_ref[0]


def _modulation(cvec, w_mod, b_mod):
    depth, d, n = w_mod.shape
    tn = 1536
    return pl.pallas_call(
        _mod_kernel,
        grid=(depth, n // tn),
        in_specs=[
            pl.BlockSpec((MOD_ROWS, d), lambda l, j: (0, 0)),
            pl.BlockSpec((1, d, tn), lambda l, j: (l, 0, j)),
            pl.BlockSpec((1, 1, tn), lambda l, j: (l, 0, j)),
        ],
        out_specs=pl.BlockSpec((1, MOD_ROWS, tn), lambda l, j: (l, 0, j)),
        out_shape=jax.ShapeDtypeStruct((depth, MOD_ROWS, n), F32),
        compiler_params=_cparams(("arbitrary", "arbitrary")),
        name="modulation",
    )(cvec, w_mod, b_mod.reshape(depth, 1, n))


def _in_kernel(x_ref, mod_ref, g_ref, w_ref, cos_ref, sin_ref, o_ref, *, n_batch, tiles_per_batch,
               rope_chunks):
    d = D_MODEL
    mrow = _mod_row(mod_ref, pl.program_id(0), n_batch, tiles_per_batch)
    sh = mrow[:, 0:d]
    sc = mrow[:, d:2 * d]
    h = (_rmsnorm_rows(x_ref[...], g_ref[...]) * (1.0 + sc) + sh).astype(BF16)
    lane = lax.broadcasted_iota(jnp.int32, (1, LANES), 1)
    first = (lane % 32) < 16
    for c, rope in enumerate(rope_chunks):
        cols = slice(c * 512, (c + 1) * 512)
        p = _dot(h, w_ref[:, cols])
        if rope:
            cos = cos_ref[...]
            sin = sin_ref[...]
            parts = []
            for s in range(512 // LANES):
                ps = p[:, s * LANES:(s + 1) * LANES]
                rot = jnp.where(first, pltpu.roll(ps, LANES - 16, 1), pltpu.roll(ps, 16, 1))
                parts.append(ps * cos + rot * sin)
            p = jnp.concatenate(parts, axis=1)
        o_ref[:, cols] = p.astype(BF16)


def _in_proj(x, mod, g, w, cos, sin, rope_chunks, n_batch, tiles_per_batch):
    t, d = x.shape
    n = w.shape[1]
    kern = functools.partial(_in_kernel, n_batch=n_batch, tiles_per_batch=tiles_per_batch,
                             rope_chunks=rope_chunks)
    return pl.pallas_call(
        kern,
        grid=(t // TOK_TILE,),
        in_specs=[
            pl.BlockSpec((TOK_TILE, d), lambda i: (i, 0)),
            pl.BlockSpec(mod.shape, lambda i: (0, 0)),
            pl.BlockSpec((1, d), lambda i: (0, 0)),
            pl.BlockSpec((d, n), lambda i: (0, 0)),
            pl.BlockSpec((TOK_TILE, LANES), lambda i: (i, 0)),
            pl.BlockSpec((TOK_TILE, LANES), lambda i: (i, 0)),
        ],
        out_specs=pl.BlockSpec((TOK_TILE, n), lambda i: (i, 0)),
        out_shape=jax.ShapeDtypeStruct((t, n), BF16),
        compiler_params=_cparams(("arbitrary",)),
        name="in_proj",
    )(x, mod, g.reshape(1, d), w, cos, sin)


def _half_masks():
    lane = lax.broadcasted_iota(jnp.int32, (1, LANES), 1)
    lo = lane < HEAD_DIM
    return lo, jnp.logical_not(lo)


def _softmax_pv(score_parts, v_parts, extra_logit=None):
    m = None
    for s in score_parts:
        mm = jnp.max(s, axis=-1, keepdims=True)
        m = mm if m is None else jnp.maximum(m, mm)
    if extra_logit is not None:
        m = jnp.maximum(m, extra_logit)
    l = None
    o = None
    for s, v in zip(score_parts, v_parts):
        p = jnp.exp(s - m)
        ll = jnp.sum(p, axis=-1, keepdims=True)
        oo = _dot(p.astype(BF16), v)
        l = ll if l is None else l + ll
        o = oo if o is None else o + oo
    if extra_logit is not None:
        l = l + jnp.exp(extra_logit - m)
    return o / l


def _na_kernel(q_ref, k0_ref, k1_ref, k2_ref, v0_ref, v1_ref, v2_ref, kc_ref, vc_ref, tab_ref, o_ref):
    lo, hi = _half_masks()
    k_refs = (k0_ref, k1_ref, k2_ref)
    v_refs = (v0_ref, v1_ref, v2_ref)
    for s in range(NA_HEADS // 2):
        sl = slice(s * LANES, (s + 1) * LANES)
        qs = q_ref[:, sl]
        outs = []
        for half, msk in enumerate((lo, hi)):
            qm = jnp.where(msk, qs, jnp.zeros_like(qs))
            head = 2 * s + half
            scores = [_dot_nt(qm, k_refs[w][:, sl])
                      + tab_ref[0, head, :, w * ATT_BLOCK:(w + 1) * ATT_BLOCK]
                      for w in range(NA_WIN_BLOCKS)]
            scores.append(_dot_nt(qm, kc_ref[:, sl]))
            vals = [v_refs[w][:, sl] for w in range(NA_WIN_BLOCKS)] + [vc_ref[:, sl]]
            outs.append(_softmax_pv(scores, vals))
        o_ref[:, sl] = jnp.where(lo, outs[0], outs[1]).astype(BF16)


def _na_table(rpb, rows):
    rpb = rpb.astype(F32)
    qi = jnp.arange(ATT_BLOCK)
    kj = jnp.arange(NA_WIN_BLOCKS * ATT_BLOCK)
    qr, qc = qi // GRID_W, qi % GRID_W
    kr, kc = kj // GRID_W, kj % GRID_W
    cs = jnp.clip(qc - NA_WIN_W // 2, 0, GRID_W - NA_WIN_W)
    valid_c = (kc[None, :] >= cs[:, None]) & (kc[None, :] < cs[:, None] + NA_WIN_W)
    dc = jnp.clip(kc[None, :] - qc[:, None] + (NA_WIN_W - 1), 0, 2 * NA_WIN_W - 2)
    win_rows = NA_WIN_BLOCKS * NA_ROWS_PER_BLOCK

    def case(r0, base_row):
        r = r0 + qr
        rs = jnp.clip(r - NA_WIN_H // 2, 0, rows - NA_WIN_H)
        krow = base_row + kr
        valid_r = (krow[None, :] >= rs[:, None]) & (krow[None, :] < rs[:, None] + NA_WIN_H)
        dr = jnp.clip(krow[None, :] - r[:, None] + (NA_WIN_H - 1), 0, 2 * NA_WIN_H - 2)
        bias = rpb[:, dr, dc]
        return jnp.where((valid_r & valid_c)[None], bias, NEG_INF)

    masked = jnp.full((rpb.shape[0], ATT_BLOCK, NA_WIN_BLOCKS * ATT_BLOCK), NEG_INF, F32)
    return jnp.stack([
        masked,
        case(0, 0),
        case(NA_ROWS_PER_BLOCK, 0),
        case(rows - NA_ROWS_PER_BLOCK, rows - win_rows),
    ])


def _na_attention(qkv, table, n_batch, lat_blocks):
    t = qkv.shape[0]
    nb, lb = n_batch, lat_blocks

    def q_row(b, j):
        return jnp.where(j == 0, b, nb + b * lb + jnp.maximum(j - 1, 0))

    def win_row(w):
        def f(b, j):
            return nb + b * lb + jnp.clip(j - 2, 0, lb - NA_WIN_BLOCKS) + w
        return f

    def case(b, j):
        return jnp.where(j == 0, 0, jnp.where(j == 1, 1, jnp.where(j == lb, 3, 2)))

    blk = (ATT_BLOCK, 512)
    in_specs = [pl.BlockSpec(blk, lambda b, j: (q_row(b, j), 0))]
    in_specs += [pl.BlockSpec(blk, lambda b, j, w=w: (win_row(w)(b, j), 1)) for w in range(NA_WIN_BLOCKS)]
    in_specs += [pl.BlockSpec(blk, lambda b, j, w=w: (win_row(w)(b, j), 2)) for w in range(NA_WIN_BLOCKS)]
    in_specs += [pl.BlockSpec(blk, lambda b, j: (b, 1)), pl.BlockSpec(blk, lambda b, j: (b, 2))]
    in_specs += [pl.BlockSpec((1,) + table.shape[1:], lambda b, j: (case(b, j), 0, 0, 0))]
    return pl.pallas_call(
        _na_kernel,
        grid=(nb, lb + 1),
        in_specs=in_specs,
        out_specs=pl.BlockSpec(blk, lambda b, j: (q_row(b, j), 0)),
        out_shape=jax.ShapeDtypeStruct((t, 512), BF16),
        compiler_params=_cparams(("arbitrary", "arbitrary")),
        name="na_attention",
    )(qkv, qkv, qkv, qkv, qkv, qkv, qkv, qkv, qkv, table)


def _diff_kernel(*refs, latent, lam_init):
    if latent:
        q_ref, kc_ref, vc_ref, k_ref, v_ref, lam_ref, g_ref, o_ref, m_scr, l_scr, acc_scr = refs
        t = pl.program_id(2)
        last = pl.num_programs(2) - 1
    else:
        q_ref, kc_ref, vc_ref, lam_ref, g_ref, _, o_ref, m_scr, l_scr, acc_scr = refs
    lo, hi = _half_masks()

    def process(kr, vr):
        tk = kr.shape[0]
        for h in range(DIFF_HEADS):
            sl = slice(h * LANES, (h + 1) * LANES)
            qs = q_ref[:, sl]
            ks = kr[:, sl]
            vs = vr[:, sl]
            for mp, msk in enumerate((lo, hi)):
                idx = 2 * h + mp
                qm = jnp.where(msk, qs, jnp.zeros_like(qs))
                s = _dot_nt(qm, ks)
                m_prev = m_scr[idx]
                m_new = jnp.maximum(m_prev, jnp.max(s, axis=-1, keepdims=True))
                alpha = jnp.exp(m_prev - m_new)
                p = jnp.exp(s - _rep_lanes(m_new, tk // LANES))
                l_scr[idx] = alpha * l_scr[idx] + jnp.sum(p, axis=-1, keepdims=True)
                acc_scr[idx] = alpha * acc_scr[idx] + _dot(p.astype(BF16), vs)
                m_scr[idx] = m_new

    def init():
        m_scr[...] = jnp.full(m_scr.shape, NEG_INF, F32)
        l_scr[...] = jnp.zeros(l_scr.shape, F32)
        acc_scr[...] = jnp.zeros(acc_scr.shape, F32)

    def finalize():
        lp = lam_ref[...]
        lam = (jnp.exp(jnp.sum(lp[0:1] * lp[1:2], axis=-1, keepdims=True))
               - jnp.exp(jnp.sum(lp[2:3] * lp[3:4], axis=-1, keepdims=True)) + lam_init)
        g = g_ref[...]
        for h in range(DIFF_HEADS):
            o = acc_scr[2 * h] / l_scr[2 * h] - lam * (acc_scr[2 * h + 1] / l_scr[2 * h + 1])
            y = _rmsnorm_rows(o, g) * (1.0 - lam_init)
            o_ref[:, h * LANES:(h + 1) * LANES] = y.astype(BF16)

    if latent:
        @pl.when(t == 0)
        def _():
            init()
            process(kc_ref, vc_ref)

        process(k_ref, v_ref)

        @pl.when(t == last)
        def _():
            finalize()
    else:
        init()
        process(kc_ref, vc_ref)
        finalize()


def _diff_attention(qkv, lam_p, subln_g, lam_init, n_batch, seq):
    t = qkv.shape[0]
    nb = n_batch
    nq = seq // DIFF_TQ
    nk = seq // DIFF_TK
    ctx_tiles_q = nb * CTX_LEN // DIFF_TQ
    ctx_tiles_k = nb * CTX_LEN // DIFF_TK
    scratch = lambda tq: [pltpu.VMEM((2 * DIFF_HEADS, tq, LANES), F32),
                          pltpu.VMEM((2 * DIFF_HEADS, tq, LANES), F32),
                          pltpu.VMEM((2 * DIFF_HEADS, tq, LANES), F32)]
    small = [pl.BlockSpec(lam_p.shape, lambda *a: (0, 0)), pl.BlockSpec((1, LANES), lambda *a: (0, 0))]
    g2 = subln_g.reshape(1, LANES)
    out = pl.pallas_call(
        functools.partial(_diff_kernel, latent=True, lam_init=lam_init),
        grid=(nb, nq, nk),
        in_specs=[
            pl.BlockSpec((DIFF_TQ, 512), lambda b, j, k: (ctx_tiles_q + b * nq + j, 3)),
            pl.BlockSpec((CTX_LEN, 512), lambda b, j, k: (b, 4)),
            pl.BlockSpec((CTX_LEN, 512), lambda b, j, k: (b, 5)),
            pl.BlockSpec((DIFF_TK, 512), lambda b, j, k: (ctx_tiles_k + b * nk + k, 4)),
            pl.BlockSpec((DIFF_TK, 512), lambda b, j, k: (ctx_tiles_k + b * nk + k, 5)),
        ] + small,
        out_specs=pl.BlockSpec((DIFF_TQ, 512), lambda b, j, k: (ctx_tiles_q + b * nq + j, 0)),
        out_shape=jax.ShapeDtypeStruct((t, 512), BF16),
        scratch_shapes=scratch(DIFF_TQ),
        compiler_params=_cparams(("arbitrary", "arbitrary", "arbitrary")),
        name="diff_attention",
    )(qkv, qkv, qkv, qkv, qkv, lam_p, g2)
    return pl.pallas_call(
        functools.partial(_diff_kernel, latent=False, lam_init=lam_init),
        grid=(nb,),
        in_specs=[
            pl.BlockSpec((CTX_LEN, 512), lambda b: (b, 3)),
            pl.BlockSpec((CTX_LEN, 512), lambda b: (b, 4)),
            pl.BlockSpec((CTX_LEN, 512), lambda b: (b, 5)),
        ] + small + [pl.BlockSpec(memory_space=pl.ANY)],
        out_specs=pl.BlockSpec((CTX_LEN, 512), lambda b: (b, 0)),
        out_shape=jax.ShapeDtypeStruct((t, 512), BF16),
        scratch_shapes=scratch(CTX_LEN),
        input_output_aliases={5: 0},
        compiler_params=_cparams(("arbitrary",)),
        name="diff_attention_ctx",
    )(qkv, qkv, qkv, lam_p, g2, out)


def _swa_kernel(sink_ref, q_ref, k0_ref, k1_ref, k2_ref, v0_ref, v1_ref, v2_ref, kc_ref, vc_ref, o_ref,
                *, seq):
    j = pl.program_id(1)
    jl = j - 1
    lo, hi = _half_masks()
    k_refs = (k0_ref, k1_ref, k2_ref)
    v_refs = (v0_ref, v1_ref, v2_ref)
    qpos = jl * ATT_BLOCK + lax.broadcasted_iota(jnp.int32, (ATT_BLOCK, ATT_BLOCK), 0)
    kiota = lax.broadcasted_iota(jnp.int32, (ATT_BLOCK, ATT_BLOCK), 1)
    valid = []
    for w in range(3):
        kpos = (jl - 1 + w) * ATT_BLOCK + kiota
        valid.append((jnp.abs(kpos - qpos) <= SWA_WINDOW) & (kpos >= 0) & (kpos < seq) & (j >= 1))
    group = SWA_Q_HEADS // SWA_KV_HEADS
    for s in range(SWA_Q_HEADS // 2):
        sl = slice(s * LANES, (s + 1) * LANES)
        n = (2 * s) // group
        kvsl = slice(n * LANES, (n + 1) * LANES)
        qs = q_ref[:, sl]
        outs = []
        for half, msk in enumerate((lo, hi)):
            qm = jnp.where(msk, qs, jnp.zeros_like(qs))
            scores = [jnp.where(valid[w], _dot_nt(qm, k_refs[w][:, kvsl]), NEG_INF) for w in range(3)]
            scores.append(_dot_nt(qm, kc_ref[:, kvsl]))
            vals = [v_refs[w][:, kvsl] for w in range(3)] + [vc_ref[:, kvsl]]
            sink = jnp.full((1, 1), sink_ref[2 * s + half], F32)
            outs.append(_softmax_pv(scores, vals, extra_logit=sink))
        o_ref[:, sl] = jnp.where(lo, outs[0], outs[1]).astype(BF16)


def _swa_attention(qkv, sink, n_batch, lat_blocks, seq):
    t = qkv.shape[0]
    nb, lb = n_batch, lat_blocks

    def q_row(b, j):
        return jnp.where(j == 0, b, nb + b * lb + jnp.maximum(j - 1, 0))

    def win_row(w):
        def f(b, j):
            return nb + b * lb + jnp.clip(j - 2 + w, 0, lb - 1)
        return f

    qblk = (ATT_BLOCK, 1024)
    kblk = (ATT_BLOCK, 512)
    in_specs = [pl.BlockSpec(memory_space=pltpu.SMEM),
                pl.BlockSpec(qblk, lambda b, j: (q_row(b, j), 0))]
    in_specs += [pl.BlockSpec(kblk, lambda b, j, w=w: (win_row(w)(b, j), 2)) for w in range(3)]
    in_specs += [pl.BlockSpec(kblk, lambda b, j, w=w: (win_row(w)(b, j), 3)) for w in range(3)]
    in_specs += [pl.BlockSpec(kblk, lambda b, j: (b, 2)), pl.BlockSpec(kblk, lambda b, j: (b, 3))]
    return pl.pallas_call(
        functools.partial(_swa_kernel, seq=seq),
        grid=(nb, lb + 1),
        in_specs=in_specs,
        out_specs=pl.BlockSpec(qblk, lambda b, j: (q_row(b, j), 0)),
        out_shape=jax.ShapeDtypeStruct((t, 1024), BF16),
        compiler_params=_cparams(("arbitrary", "arbitrary")),
        name="swa_attention",
    )(sink, qkv, qkv, qkv, qkv, qkv, qkv, qkv, qkv, qkv)


def _out_kernel(*refs, n_parts, n_batch, tiles_per_batch):
    o_refs = refs[:n_parts]
    w_refs = refs[n_parts:2 * n_parts]
    x_ref, mod_ref, g_ref, wq_ref, keys_ref, x1_ref, h2_ref, st_ref = refs[2 * n_parts:]
    d = D_MODEL
    y = None
    for o_ref, w_ref in zip(o_refs, w_refs):
        yy = _dot(o_ref[...], w_ref[...])
        y = yy if y is None else y + yy
    mrow = _mod_row(mod_ref, pl.program_id(0), n_batch, tiles_per_batch)
    g1 = mrow[:, 2 * d:3 * d]
    sh2 = mrow[:, 3 * d:4 * d]
    sc2 = mrow[:, 4 * d:5 * d]
    x1 = x_ref[...] + g1 * y
    x1_ref[...] = x1
    h = (_rmsnorm_rows(x1, g_ref[...]) * (1.0 + sc2) + sh2).astype(BF16)
    h2_ref[...] = h
    for hp in range(2 * PEER_HEADS):
        qh = _dot(h, wq_ref[:, hp * LANES:(hp + 1) * LANES]).astype(BF16)
        st_ref[0, hp] = _dot_nt(keys_ref[hp], qh)


def _out_proj(o_parts, w_parts, x, mod, g, wq, keys, n_batch, tiles_per_batch):
    t, d = x.shape
    nt = t // TOK_TILE
    n_parts = len(o_parts)
    in_specs = [pl.BlockSpec((TOK_TILE, o.shape[1]), lambda i: (i, 0)) for o in o_parts]
    in_specs += [pl.BlockSpec(w.shape, lambda i: (0, 0)) for w in w_parts]
    in_specs += [
        pl.BlockSpec((TOK_TILE, d), lambda i: (i, 0)),
        pl.BlockSpec(mod.shape, lambda i: (0, 0)),
        pl.BlockSpec((1, d), lambda i: (0, 0)),
        pl.BlockSpec(wq.shape, lambda i: (0, 0)),
        pl.BlockSpec(keys.shape, lambda i: (0, 0, 0)),
    ]
    kern = functools.partial(_out_kernel, n_parts=n_parts, n_batch=n_batch, tiles_per_batch=tiles_per_batch)
    return pl.pallas_call(
        kern,
        grid=(nt,),
        in_specs=in_specs,
        out_specs=[
            pl.BlockSpec((TOK_TILE, d), lambda i: (i, 0)),
            pl.BlockSpec((TOK_TILE, d), lambda i: (i, 0)),
            pl.BlockSpec((1, 2 * PEER_HEADS, PEER_NKEYS, TOK_TILE), lambda i: (i, 0, 0, 0)),
        ],
        out_shape=[
            jax.ShapeDtypeStruct((t, d), F32),
            jax.ShapeDtypeStruct((t, d), BF16),
            jax.ShapeDtypeStruct((nt, 2 * PEER_HEADS, PEER_NKEYS, TOK_TILE), F32),
        ],
        compiler_params=_cparams(("arbitrary",)),
        name="out_proj",
    )(*o_parts, *w_parts, x, mod, g.reshape(1, d), wq, keys)


def _rank_topk(a, sorted_ref):
    iota = lax.broadcasted_iota(jnp.int32, a.shape, 0)

    def body(r, carry):
        cur, rank = carry
        m = jnp.max(cur, axis=0, keepdims=True)
        first = jnp.min(jnp.where(cur == m, iota, PEER_NKEYS), axis=0, keepdims=True)
        sel = iota == first
        sorted_ref[pl.ds(r, 1), :] = m
        return jnp.where(sel, -jnp.inf, cur), jnp.where(sel, r.astype(F32), rank)

    _, rank = lax.fori_loop(0, PEER_TOPK, body, (a, jnp.full(a.shape, NOT_TOPK, F32)))
    return rank


def _route_kernel(st_ref, r2_ref, eb_ref, nn_ref, ea_ref, sa_ref, sb_ref):
    k = PEER_TOPK
    chunk = LANES
    riota = lax.broadcasted_iota(jnp.int32, (k, chunk), 0)
    for c in range(TOK_TILE // chunk):
        cols = slice(c * chunk, (c + 1) * chunk)
        a = st_ref[0, 0, :, cols]
        b = st_ref[0, 1, :, cols]
        ra = _rank_topk(a, sa_ref)
        rb = _rank_topk(b, sb_ref)
        sa = sa_ref[...]
        sb = sb_ref[...]

        def walk(_, n):
            sbn = jnp.full((k, chunk), -jnp.inf, F32)
            for r2 in range(k):
                sbn = jnp.where(n == r2, sb[r2:r2 + 1, :], sbn)
            f = sa + sbn
            m = jnp.max(f, axis=0, keepdims=True)
            first = jnp.min(jnp.where(f == m, riota, k), axis=0, keepdims=True)
            return n + (riota == first).astype(jnp.int32)

        n = lax.fori_loop(0, k, walk, jnp.zeros((k, chunk), jnp.int32))
        ea_s = jnp.exp(sa - sa[0:1, :])
        eb_s = jnp.exp(sb - sb[0:1, :])
        z = jnp.zeros((1, chunk), F32)
        for r2 in range(k):
            z = z + eb_s[r2:r2 + 1, :] * jnp.sum(jnp.where(n > r2, ea_s, 0.0), axis=0, keepdims=True)
        nf = n.astype(F32)
        nn = jnp.zeros(a.shape, F32)
        for r1 in range(k):
            nn = jnp.where(ra == float(r1), nf[r1:r1 + 1, :], nn)
        r2_ref[0, 0, :, cols] = rb
        eb_ref[0, 0, :, cols] = jnp.exp(b - sb[0:1, :])
        nn_ref[0, 0, :, cols] = nn
        ea_ref[0, 0, :, cols] = jnp.where(ra < float(k), jnp.exp(a - sa[0:1, :]), 0.0) / z


def _route(st):
    nt = st.shape[0]
    tab = jax.ShapeDtypeStruct((nt, PEER_HEADS, PEER_NKEYS, TOK_TILE), F32)
    spec = pl.BlockSpec((1, 1, PEER_NKEYS, TOK_TILE), lambda i, h: (i, h, 0, 0))
    return pl.pallas_call(
        _route_kernel,
        grid=(nt, PEER_HEADS),
        in_specs=[pl.BlockSpec((1, 2, PEER_NKEYS, TOK_TILE), lambda i, h: (i, h, 0, 0))],
        out_specs=[spec, spec, spec, spec],
        out_shape=[tab, tab, tab, tab],
        scratch_shapes=[pltpu.VMEM((PEER_TOPK, LANES), F32), pltpu.VMEM((PEER_TOPK, LANES), F32)],
        compiler_params=_cparams(("arbitrary", "arbitrary")),
        name="peer_route",
    )(st)


def _peer_kernel(h_ref, u_ref, vt_ref, r2_ref, eb_ref, nn_ref, ea_ref, x_ref, mod_ref, o_ref,
                 acc_ref, ga_ref, *, n_batch, tiles_per_batch):
    j = pl.program_id(1)
    d = D_MODEL

    @pl.when(j == 0)
    def _():
        acc_ref[...] = jnp.zeros(acc_ref.shape, F32)

    h = h_ref[...]
    per_step = PEER_TE // PEER_NKEYS
    for q in range(per_step):
        i1 = j * per_step + q
        rows = slice(q * PEER_NKEYS, (q + 1) * PEER_NKEYS)
        st = _dot_nt(u_ref[rows, :], h)
        act = 0.5 * st * (1.0 + lax.erf(st * (1.0 / math.sqrt(2.0))))
        gate = jnp.zeros(st.shape, F32)
        for hd in range(PEER_HEADS):
            n_row = nn_ref[0, hd, pl.ds(i1, 1), :]
            ea_row = ea_ref[0, hd, pl.ds(i1, 1), :]
            gate = gate + jnp.where(r2_ref[0, hd] < n_row, ea_row * eb_ref[0, hd], 0.0)
        ga_ref[rows, :] = (gate * act).astype(BF16)
    acc_ref[...] += _dot(vt_ref[...], ga_ref[...])

    @pl.when(j == pl.num_programs(1) - 1)
    def _():
        mrow = _mod_row(mod_ref, pl.program_id(0), n_batch, tiles_per_batch)
        g2 = mrow[:, 5 * d:6 * d]
        o_ref[...] = x_ref[...] + g2 * acc_ref[...].T


def _peer(h2, u, vt, tables, x1, mod, n_batch, tiles_per_batch):
    t, d = x1.shape
    n_exp = u.shape[0]
    tab_spec = pl.BlockSpec((1, PEER_HEADS, PEER_NKEYS, TOK_TILE), lambda i, j: (i, 0, 0, 0))
    kern = functools.partial(_peer_kernel, n_batch=n_batch, tiles_per_batch=tiles_per_batch)
    return pl.pallas_call(
        kern,
        grid=(t // TOK_TILE, n_exp // PEER_TE),
        in_specs=[
            pl.BlockSpec((TOK_TILE, d), lambda i, j: (i, 0)),
            pl.BlockSpec((PEER_TE, d), lambda i, j: (j, 0)),
            pl.BlockSpec((d, PEER_TE), lambda i, j: (0, j)),
            tab_spec, tab_spec, tab_spec, tab_spec,
            pl.BlockSpec((TOK_TILE, d), lambda i, j: (i, 0)),
            pl.BlockSpec(mod.shape, lambda i, j: (0, 0)),
        ],
        out_specs=pl.BlockSpec((TOK_TILE, d), lambda i, j: (i, 0)),
        out_shape=jax.ShapeDtypeStruct((t, d), F32),
        scratch_shapes=[pltpu.VMEM((d, TOK_TILE), F32), pltpu.VMEM((PEER_TE, TOK_TILE), BF16)],
        compiler_params=_cparams(("arbitrary", "arbitrary")),
        name="peer_experts",
    )(h2, u, vt, *tables, x1, mod)


def _final_kernel(x_ref, g_ref, o_ref):
    o_ref[...] = _rmsnorm_rows(x_ref[...], g_ref[...])


def _final_norm(x, g, first_tile, n_tiles):
    d = x.shape[1]
    return pl.pallas_call(
        _final_kernel,
        grid=(n_tiles,),
        in_specs=[pl.BlockSpec((TOK_TILE, d), lambda i: (first_tile + i, 0)),
                  pl.BlockSpec((1, d), lambda i: (0, 0))],
        out_specs=pl.BlockSpec((TOK_TILE, d), lambda i: (i, 0)),
        out_shape=jax.ShapeDtypeStruct((n_tiles * TOK_TILE, d), F32),
        compiler_params=_cparams(("arbitrary",)),
        name="final_norm",
    )(x, g.reshape(1, d))


def _rope_tables(n_batch, seq):
    pos = jnp.arange(seq)
    row = (pos // GRID_W).astype(F32)
    col = (pos % GRID_W).astype(F32)
    half = HEAD_DIM // 2
    inv = ROPE_THETA ** (-jnp.arange(0, half, 2, dtype=F32) / half)
    ar = row[:, None] * inv
    ac = col[:, None] * inv
    ang = jnp.concatenate([ar, ar, ac, ac], axis=-1)
    sign = jnp.where((jnp.arange(HEAD_DIM) % 32) < 16, -1.0, 1.0).astype(F32)
    cos = jnp.tile(jnp.cos(ang), (n_batch, LANES // HEAD_DIM))
    sin = jnp.tile(jnp.sin(ang) * sign, (n_batch, LANES // HEAD_DIM))
    n_ctx = n_batch * CTX_LEN
    cos = jnp.concatenate([jnp.ones((n_ctx, LANES), F32), cos], axis=0)
    sin = jnp.concatenate([jnp.zeros((n_ctx, LANES), F32), sin], axis=0)
    return cos, sin


def _even_w_in(w):
    a = NA_HEADS * HEAD_DIM
    scale = jnp.ones((w.shape[1],), F32)
    scale = scale.at[0:a].set(HEAD_DIM ** -0.5)
    scale = scale.at[3 * a:3 * a + 512].set(HEAD_DIM ** -0.5)
    return (w * scale).astype(BF16)


def _odd_w_in(w):
    nq = SWA_Q_HEADS * HEAD_DIM
    nkv = SWA_KV_HEADS * HEAD_DIM
    d = w.shape[0]
    q = w[:, :nq] * (HEAD_DIM ** -0.5)
    k = w[:, nq:nq + nkv].reshape(d, SWA_KV_HEADS, 1, HEAD_DIM)
    v = w[:, nq + nkv:].reshape(d, SWA_KV_HEADS, 1, HEAD_DIM)
    k2 = jnp.broadcast_to(k, (d, SWA_KV_HEADS, 2, HEAD_DIM)).reshape(d, 2 * nkv)
    v2 = jnp.broadcast_to(v, (d, SWA_KV_HEADS, 2, HEAD_DIM)).reshape(d, 2 * nkv)
    return jnp.concatenate([q, k2, v2], axis=1).astype(BF16)


def kernel(x, c, ctx, c_ctx, w_mod, b_mod, norm1_g, norm2_g, w_in_even, w_out_even, na_rpb, diff_lambda,
           diff_subln_g, w_in_odd, w_out_odd, swa_sink, peer_wq, peer_keys, peer_u, peer_v, final_g):
    n_batch, seq, d = x.shape
    depth = w_mod.shape[0]
    assert d == D_MODEL and ctx.shape[1] == CTX_LEN and (n_batch * CTX_LEN) == TOK_TILE
    assert seq % TOK_TILE == 0 and seq // ATT_BLOCK >= NA_WIN_BLOCKS
    tiles_per_batch = seq // TOK_TILE
    lat_blocks = seq // ATT_BLOCK
    rows = seq // GRID_W

    xt = jnp.concatenate([ctx.reshape(n_batch * CTX_LEN, d), x.reshape(n_batch * seq, d)], axis=0)
    cvec = jnp.concatenate([c, c_ctx[None, :],
                            jnp.zeros((MOD_ROWS - n_batch - 1, d), F32)], axis=0)
    mod = _modulation(cvec, w_mod, b_mod)
    cos, sin = _rope_tables(n_batch, seq)

    for i in range(depth):
        j = i // 2
        if i % 2 == 0:
            lam_init = 0.8 - 0.6 * math.exp(-0.3 * i)
            qkv = _in_proj(xt, mod[i], norm1_g[i], _even_w_in(w_in_even[j]), cos, sin,
                           (False, False, False, True, True, False), n_batch, tiles_per_batch)
            oa = _na_attention(qkv, _na_table(na_rpb[j], rows), n_batch, lat_blocks)
            ob = _diff_attention(qkv, diff_lambda[j], diff_subln_g[j], lam_init, n_batch, seq)
            w_out = w_out_even[j].astype(BF16)
            o_parts, w_parts = [oa, ob], [w_out[:512], w_out[512:]]
        else:
            qkv = _in_proj(xt, mod[i], norm1_g[i], _odd_w_in(w_in_odd[j]), cos, sin,
                           (True, True, True, False), n_batch, tiles_per_batch)
            o = _swa_attention(qkv, swa_sink[j], n_batch, lat_blocks, seq)
            o_parts, w_parts = [o], [w_out_odd[j].astype(BF16)]
        keys = peer_keys[i].reshape(2 * PEER_HEADS, PEER_NKEYS, PEER_NKEYS).astype(BF16)
        x1, h2, st = _out_proj(o_parts, w_parts, xt, mod[i], norm2_g[i], peer_wq[i].astype(BF16), keys,
                               n_batch, tiles_per_batch)
        tables = _route(st)
        xt = _peer(h2, peer_u[i].astype(BF16), peer_v[i].T.astype(BF16), tables, x1, mod[i],
                   n_batch, tiles_per_batch)

    out = _final_norm(xt, final_g, n_batch * CTX_LEN // TOK_TILE, n_batch * seq // TOK_TILE)
    return out.reshape(n_batch, seq, d)
```

```python
import functools
import math

import jax
import jax.numpy as jnp
import numpy as np
from jax import lax
from jax.experimental import pallas as pl
from jax.experimental.pallas import tpu as pltpu

F32 = jnp.float32
BF16 = jnp.bfloat16

D_MODEL = 1024
CTX_LEN = 256
GRID_W = 64
HEAD_DIM = 64
ROPE_THETA = 10000.0
EPS = 1e-6
NEG_INF = -1e30

NA_HEADS = 8
NA_WIN_H = 8
NA_WIN_W = 16
DIFF_HEADS = 4
SWA_Q_HEADS = 16
SWA_KV_HEADS = 4
SWA_WINDOW = 128
PEER_HEADS = 8
PEER_NKEYS = 128
PEER_TOPK = 16

LANES = 128
TOK_TILE = 512
ATT_BLOCK = 256
NA_ROWS_PER_BLOCK = ATT_BLOCK // GRID_W
NA_WIN_BLOCKS = 3
DIFF_TQ = 512
DIFF_TK = 512
PEER_STEP = 1024
PEER_CHUNK = 256
PEER_SUB = 8
NN_PAD_ROWS = 8
NOT_TOPK = 999.0
MOD_ROWS = 8
VMEM_LIMIT = 56 * 1024 * 1024


def _cparams(sem):
    return pltpu.CompilerParams(dimension_semantics=sem, vmem_limit_bytes=VMEM_LIMIT)


def _dot(a, b):
    return jnp.dot(a, b, preferred_element_type=F32)


def _dot_nt(a, b):
    return lax.dot_general(a, b, (((1,), (1,)), ((), ())), preferred_element_type=F32)


def _rep_lanes(v, n):
    return v if n == 1 else jnp.concatenate([v] * n, axis=1)


def _mod_row(mod_ref, i, n_batch, tiles_per_batch):
    r = jnp.where(i == 0, n_batch, jnp.maximum(i - 1, 0) // tiles_per_batch)
    return mod_ref[pl.ds(r, 1), :]


def _rmsnorm_rows(x, g):
    ms = jnp.mean(x * x, axis=-1, keepdims=True)
    return x * lax.rsqrt(ms + EPS) * g


def _mod_kernel(cv_ref, w_ref, b_ref, o_ref):
    cv = cv_ref[...]
    s = cv * (1.0 / (1.0 + jnp.exp(-cv)))
    o_ref[0] = jnp.dot(s, w_ref[0], preferred_element_type=F32,
                       precision=lax.Precision.HIGHEST) + b_ref[0]


def _modulation(cvec, w_mod, b_mod):
    depth, d, n = w_mod.shape
    tn = 1536
    return pl.pallas_call(
        _mod_kernel,
        grid=(depth, n // tn),
        in_specs=[
            pl.BlockSpec((MOD_ROWS, d), lambda l, j: (0, 0)),
            pl.BlockSpec((1, d, tn), lambda l, j: (l, 0, j)),
            pl.BlockSpec((1, 1, tn), lambda l, j: (l, 0, j)),
        ],
        out_specs=pl.BlockSpec((1, MOD_ROWS, tn), lambda l, j: (l, 0, j)),
        out_shape=jax.ShapeDtypeStruct((depth, MOD_ROWS, n), F32),
        compiler_params=_cparams(("arbitrary", "arbitrary")),
        name="modulation",
    )(cvec, w_mod, b_mod.reshape(depth, 1, n))


def _in_kernel(x_ref, mod_ref, g_ref, w_ref, cos_ref, sin_ref, o_ref, *, n_batch, tiles_per_batch,
               rope_chunks):
    d = D_MODEL
    mrow = _mod_row(mod_ref, pl.program_id(0), n_batch, tiles_per_batch)
    sh = mrow[:, 0:d]
    sc = mrow[:, d:2 * d]
    h = (_rmsnorm_rows(x_ref[...], g_ref[...]) * (1.0 + sc) + sh).astype(BF16)
    lane = lax.broadcasted_iota(jnp.int32, (1, LANES), 1)
    first = (lane % 32) < 16
    for c, rope in enumerate(rope_chunks):
        cols = slice(c * 512, (c + 1) * 512)
        p = _dot(h, w_ref[:, cols])
        if rope:
            cos = cos_ref[...]
            sin = sin_ref[...]
            parts = []
            for s in range(512 // LANES):
                ps = p[:, s * LANES:(s + 1) * LANES]
                rot = jnp.where(first, pltpu.roll(ps, LANES - 16, 1), pltpu.roll(ps, 16, 1))
                parts.append(ps * cos + rot * sin)
            p = jnp.concatenate(parts, axis=1)
        o_ref[:, cols] = p.astype(BF16)


def _in_proj(x, mod, g, w, cos, sin, rope_chunks, n_batch, tiles_per_batch):
    t, d = x.shape
    n = w.shape[1]
    kern = functools.partial(_in_kernel, n_batch=n_batch, tiles_per_batch=tiles_per_batch,
                             rope_chunks=rope_chunks)
    return pl.pallas_call(
        kern,
        grid=(t // TOK_TILE,),
        in_specs=[
            pl.BlockSpec((TOK_TILE, d), lambda i: (i, 0)),
            pl.BlockSpec(mod.shape, lambda i: (0, 0)),
            pl.BlockSpec((1, d), lambda i: (0, 0)),
            pl.BlockSpec((d, n), lambda i: (0, 0)),
            pl.BlockSpec((TOK_TILE, LANES), lambda i: (i, 0)),
            pl.BlockSpec((TOK_TILE, LANES), lambda i: (i, 0)),
        ],
        out_specs=pl.BlockSpec((TOK_TILE, n), lambda i: (i, 0)),
        out_shape=jax.ShapeDtypeStruct((t, n), BF16),
        compiler_params=_cparams(("arbitrary",)),
        name="in_proj",
    )(x, mod, g.reshape(1, d), w, cos, sin)


def _half_masks():
    lane = lax.broadcasted_iota(jnp.int32, (1, LANES), 1)
    lo = lane < HEAD_DIM
    return lo, jnp.logical_not(lo)


def _softmax_pv(score_parts, v_parts, extra_logit=None):
    m = None
    for s in score_parts:
        mm = jnp.max(s, axis=-1, keepdims=True)
        m = mm if m is None else jnp.maximum(m, mm)
    if extra_logit is not None:
        m = jnp.maximum(m, extra_logit)
    l = None
    o = None
    for s, v in zip(score_parts, v_parts):
        p = jnp.exp(s - m)
        ll = jnp.sum(p, axis=-1, keepdims=True)
        oo = _dot(p.astype(BF16), v)
        l = ll if l is None else l + ll
        o = oo if o is None else o + oo
    if extra_logit is not None:
        l = l + jnp.exp(extra_logit - m)
    return o / l


def _na_kernel(q_ref, k0_ref, k1_ref, k2_ref, v0_ref, v1_ref, v2_ref, kc_ref, vc_ref, tab_ref, o_ref):
    lo, hi = _half_masks()
    k_refs = (k0_ref, k1_ref, k2_ref)
    v_refs = (v0_ref, v1_ref, v2_ref)
    for s in range(NA_HEADS // 2):
        sl = slice(s * LANES, (s + 1) * LANES)
        qs = q_ref[:, sl]
        outs = []
        for half, msk in enumerate((lo, hi)):
            qm = jnp.where(msk, qs, jnp.zeros_like(qs))
            head = 2 * s + half
            scores = [_dot_nt(qm, k_refs[w][:, sl])
                      + tab_ref[0, head, :, w * ATT_BLOCK:(w + 1) * ATT_BLOCK]
                      for w in range(NA_WIN_BLOCKS)]
            scores.append(_dot_nt(qm, kc_ref[:, sl]))
            vals = [v_refs[w][:, sl] for w in range(NA_WIN_BLOCKS)] + [vc_ref[:, sl]]
            outs.append(_softmax_pv(scores, vals))
        o_ref[:, sl] = jnp.where(lo, outs[0], outs[1]).astype(BF16)


def _na_table(rpb, rows):
    win_rows = NA_WIN_BLOCKS * NA_ROWS_PER_BLOCK
    qc = np.arange(GRID_W)
    cs = np.clip(qc - NA_WIN_W // 2, 0, GRID_W - NA_WIN_W)
    kc = np.arange(GRID_W)
    valid_c = (kc[None, :] >= cs[:, None]) & (kc[None, :] < cs[:, None] + NA_WIN_W)
    dc = kc[None, :] - qc[:, None] + (NA_WIN_W - 1)
    sel_c = (valid_c[:, :, None] & (dc[:, :, None] == np.arange(2 * NA_WIN_W - 1))).astype(np.float32)

    def row_geometry(r0, base_row):
        r = r0 + np.arange(NA_ROWS_PER_BLOCK)
        rs = np.clip(r - NA_WIN_H // 2, 0, rows - NA_WIN_H)
        kr = base_row + np.arange(win_rows)
        valid_r = (kr[None, :] >= rs[:, None]) & (kr[None, :] < rs[:, None] + NA_WIN_H)
        dr = kr[None, :] - r[:, None] + (NA_WIN_H - 1)
        sel_r = (valid_r[:, :, None] & (dr[:, :, None] == np.arange(2 * NA_WIN_H - 1))).astype(np.float32)
        return valid_r, sel_r

    geoms = [row_geometry(0, 0), row_geometry(NA_ROWS_PER_BLOCK, 0),
             row_geometry(rows - NA_ROWS_PER_BLOCK, rows - win_rows)]
    sel_r = np.stack([np.zeros_like(geoms[0][1])] + [g[1] for g in geoms])
    valid = np.stack([np.zeros_like(geoms[0][0])] + [g[0] for g in geoms])
    valid = valid[:, :, None, :, None] & valid_c[None, None, :, None, :]
    mask = np.where(valid, 0.0, NEG_INF).astype(np.float32)
    bias = jnp.einsum("hrc,gqkr,xyc->ghqxky", rpb.astype(F32), sel_r, sel_c,
                      precision=lax.Precision.HIGHEST)
    table = bias + mask[:, None]
    return table.reshape(4, rpb.shape[0], ATT_BLOCK, NA_WIN_BLOCKS * ATT_BLOCK)


def _na_attention(qkv, table, n_batch, lat_blocks):
    t = qkv.shape[0]
    nb, lb = n_batch, lat_blocks

    def q_row(b, j):
        return jnp.where(j == 0, b, nb + b * lb + jnp.maximum(j - 1, 0))

    def win_row(w):
        def f(b, j):
            return nb + b * lb + jnp.clip(j - 2, 0, lb - NA_WIN_BLOCKS) + w
        return f

    def case(b, j):
        return jnp.where(j == 0, 0, jnp.where(j == 1, 1, jnp.where(j == lb, 3, 2)))

    blk = (ATT_BLOCK, 512)
    in_specs = [pl.BlockSpec(blk, lambda b, j: (q_row(b, j), 0))]
    in_specs += [pl.BlockSpec(blk, lambda b, j, w=w: (win_row(w)(b, j), 1)) for w in range(NA_WIN_BLOCKS)]
    in_specs += [pl.BlockSpec(blk, lambda b, j, w=w: (win_row(w)(b, j), 2)) for w in range(NA_WIN_BLOCKS)]
    in_specs += [pl.BlockSpec(blk, lambda b, j: (b, 1)), pl.BlockSpec(blk, lambda b, j: (b, 2))]
    in_specs += [pl.BlockSpec((1,) + table.shape[1:], lambda b, j: (case(b, j), 0, 0, 0))]
    return pl.pallas_call(
        _na_kernel,
        grid=(nb, lb + 1),
        in_specs=in_specs,
        out_specs=pl.BlockSpec(blk, lambda b, j: (q_row(b, j), 0)),
        out_shape=jax.ShapeDtypeStruct((t, 512), BF16),
        compiler_params=_cparams(("arbitrary", "arbitrary")),
        name="na_attention",
    )(qkv, qkv, qkv, qkv, qkv, qkv, qkv, qkv, qkv, table)


def _diff_kernel(*refs, latent, lam_init):
    if latent:
        q_ref, kc_ref, vc_ref, k_ref, v_ref, lam_ref, g_ref, o_ref, m_scr, l_scr, acc_scr = refs
        t = pl.program_id(2)
        last = pl.num_programs(2) - 1
    else:
        q_ref, kc_ref, vc_ref, lam_ref, g_ref, _, o_ref, m_scr, l_scr, acc_scr = refs
    lo, hi = _half_masks()

    def process(kr, vr):
        tk = kr.shape[0]
        for h in range(DIFF_HEADS):
            sl = slice(h * LANES, (h + 1) * LANES)
            qs = q_ref[:, sl]
            ks = kr[:, sl]
            vs = vr[:, sl]
            for mp, msk in enumerate((lo, hi)):
                idx = 2 * h + mp
                qm = jnp.where(msk, qs, jnp.zeros_like(qs))
                s = _dot_nt(qm, ks)
                m_prev = m_scr[idx]
                m_new = jnp.maximum(m_prev, jnp.max(s, axis=-1, keepdims=True))
                alpha = jnp.exp(m_prev - m_new)
                p = jnp.exp(s - _rep_lanes(m_new, tk // LANES))
                l_scr[idx] = alpha * l_scr[idx] + jnp.sum(p, axis=-1, keepdims=True)
                acc_scr[idx] = alpha * acc_scr[idx] + _dot(p.astype(BF16), vs)
                m_scr[idx] = m_new

    def init():
        m_scr[...] = jnp.full(m_scr.shape, NEG_INF, F32)
        l_scr[...] = jnp.zeros(l_scr.shape, F32)
        acc_scr[...] = jnp.zeros(acc_scr.shape, F32)

    def finalize():
        lp = lam_ref[...]
        lam = (jnp.exp(jnp.sum(lp[0:1] * lp[1:2], axis=-1, keepdims=True))
               - jnp.exp(jnp.sum(lp[2:3] * lp[3:4], axis=-1, keepdims=True)) + lam_init)
        g = g_ref[...]
        for h in range(DIFF_HEADS):
            o = acc_scr[2 * h] / l_scr[2 * h] - lam * (acc_scr[2 * h + 1] / l_scr[2 * h + 1])
            y = _rmsnorm_rows(o, g) * (1.0 - lam_init)
            o_ref[:, h * LANES:(h + 1) * LANES] = y.astype(BF16)

    if latent:
        @pl.when(t == 0)
        def _():
            init()
            process(kc_ref, vc_ref)

        process(k_ref, v_ref)

        @pl.when(t == last)
        def _():
            finalize()
    else:
        init()
        process(kc_ref, vc_ref)
        finalize()


def _diff_attention(qkv, lam_p, subln_g, lam_init, n_batch, seq):
    t = qkv.shape[0]
    nb = n_batch
    nq = seq // DIFF_TQ
    nk = seq // DIFF_TK
    ctx_tiles_q = nb * CTX_LEN // DIFF_TQ
    ctx_tiles_k = nb * CTX_LEN // DIFF_TK
    scratch = lambda tq: [pltpu.VMEM((2 * DIFF_HEADS, tq, LANES), F32),
                          pltpu.VMEM((2 * DIFF_HEADS, tq, LANES), F32),
                          pltpu.VMEM((2 * DIFF_HEADS, tq, LANES), F32)]
    small = [pl.BlockSpec(lam_p.shape, lambda *a: (0, 0)), pl.BlockSpec((1, LANES), lambda *a: (0, 0))]
    g2 = subln_g.reshape(1, LANES)
    out = pl.pallas_call(
        functools.partial(_diff_kernel, latent=True, lam_init=lam_init),
        grid=(nb, nq, nk),
        in_specs=[
            pl.BlockSpec((DIFF_TQ, 512), lambda b, j, k: (ctx_tiles_q + b * nq + j, 3)),
            pl.BlockSpec((CTX_LEN, 512), lambda b, j, k: (b, 4)),
            pl.BlockSpec((CTX_LEN, 512), lambda b, j, k: (b, 5)),
            pl.BlockSpec((DIFF_TK, 512), lambda b, j, k: (ctx_tiles_k + b * nk + k, 4)),
            pl.BlockSpec((DIFF_TK, 512), lambda b, j, k: (ctx_tiles_k + b * nk + k, 5)),
        ] + small,
        out_specs=pl.BlockSpec((DIFF_TQ, 512), lambda b, j, k: (ctx_tiles_q + b * nq + j, 0)),
        out_shape=jax.ShapeDtypeStruct((t, 512), BF16),
        scratch_shapes=scratch(DIFF_TQ),
        compiler_params=_cparams(("arbitrary", "arbitrary", "arbitrary")),
        name="diff_attention",
    )(qkv, qkv, qkv, qkv, qkv, lam_p, g2)
    return pl.pallas_call(
        functools.partial(_diff_kernel, latent=False, lam_init=lam_init),
        grid=(nb,),
        in_specs=[
            pl.BlockSpec((CTX_LEN, 512), lambda b: (b, 3)),
            pl.BlockSpec((CTX_LEN, 512), lambda b: (b, 4)),
            pl.BlockSpec((CTX_LEN, 512), lambda b: (b, 5)),
        ] + small + [pl.BlockSpec(memory_space=pl.ANY)],
        out_specs=pl.BlockSpec((CTX_LEN, 512), lambda b: (b, 0)),
        out_shape=jax.ShapeDtypeStruct((t, 512), BF16),
        scratch_shapes=scratch(CTX_LEN),
        input_output_aliases={5: 0},
        compiler_params=_cparams(("arbitrary",)),
        name="diff_attention_ctx",
    )(qkv, qkv, qkv, lam_p, g2, out)


def _swa_kernel(sink_ref, q_ref, k0_ref, k1_ref, k2_ref, v0_ref, v1_ref, v2_ref, kc_ref, vc_ref, o_ref,
                *, seq):
    j = pl.program_id(1)
    jl = j - 1
    lo, hi = _half_masks()
    k_refs = (k0_ref, k1_ref, k2_ref)
    v_refs = (v0_ref, v1_ref, v2_ref)
    qpos = jl * ATT_BLOCK + lax.broadcasted_iota(jnp.int32, (ATT_BLOCK, ATT_BLOCK), 0)
    kiota = lax.broadcasted_iota(jnp.int32, (ATT_BLOCK, ATT_BLOCK), 1)
    valid = []
    for w in range(3):
        kpos = (jl - 1 + w) * ATT_BLOCK + kiota
        valid.append((jnp.abs(kpos - qpos) <= SWA_WINDOW) & (kpos >= 0) & (kpos < seq) & (j >= 1))
    group = SWA_Q_HEADS // SWA_KV_HEADS
    for s in range(SWA_Q_HEADS // 2):
        sl = slice(s * LANES, (s + 1) * LANES)
        n = (2 * s) // group
        kvsl = slice(n * LANES, (n + 1) * LANES)
        qs = q_ref[:, sl]
        outs = []
        for half, msk in enumerate((lo, hi)):
            qm = jnp.where(msk, qs, jnp.zeros_like(qs))
            scores = [jnp.where(valid[w], _dot_nt(qm, k_refs[w][:, kvsl]), NEG_INF) for w in range(3)]
            scores.append(_dot_nt(qm, kc_ref[:, kvsl]))
            vals = [v_refs[w][:, kvsl] for w in range(3)] + [vc_ref[:, kvsl]]
            sink = jnp.full((1, 1), sink_ref[2 * s + half], F32)
            outs.append(_softmax_pv(scores, vals, extra_logit=sink))
        o_ref[:, sl] = jnp.where(lo, outs[0], outs[1]).astype(BF16)


def _swa_attention(qkv, sink, n_batch, lat_blocks, seq):
    t = qkv.shape[0]
    nb, lb = n_batch, lat_blocks

    def q_row(b, j):
        return jnp.where(j == 0, b, nb + b * lb + jnp.maximum(j - 1, 0))

    def win_row(w):
        def f(b, j):
            return nb + b * lb + jnp.clip(j - 2 + w, 0, lb - 1)
        return f

    qblk = (ATT_BLOCK, 1024)
    kblk = (ATT_BLOCK, 512)
    in_specs = [pl.BlockSpec(memory_space=pltpu.SMEM),
                pl.BlockSpec(qblk, lambda b, j: (q_row(b, j), 0))]
    in_specs += [pl.BlockSpec(kblk, lambda b, j, w=w: (win_row(w)(b, j), 2)) for w in range(3)]
    in_specs += [pl.BlockSpec(kblk, lambda b, j, w=w: (win_row(w)(b, j), 3)) for w in range(3)]
    in_specs += [pl.BlockSpec(kblk, lambda b, j: (b, 2)), pl.BlockSpec(kblk, lambda b, j: (b, 3))]
    return pl.pallas_call(
        functools.partial(_swa_kernel, seq=seq),
        grid=(nb, lb + 1),
        in_specs=in_specs,
        out_specs=pl.BlockSpec(qblk, lambda b, j: (q_row(b, j), 0)),
        out_shape=jax.ShapeDtypeStruct((t, 1024), BF16),
        compiler_params=_cparams(("arbitrary", "arbitrary")),
        name="swa_attention",
    )(sink, qkv, qkv, qkv, qkv, qkv, qkv, qkv, qkv, qkv)


def _out_kernel(*refs, n_parts, n_batch, tiles_per_batch):
    o_refs = refs[:n_parts]
    w_refs = refs[n_parts:2 * n_parts]
    x_ref, mod_ref, g_ref, wq_ref, keys_ref, x1_ref, h2_ref, st_ref = refs[2 * n_parts:]
    d = D_MODEL
    y = None
    for o_ref, w_ref in zip(o_refs, w_refs):
        yy = _dot(o_ref[...], w_ref[...])
        y = yy if y is None else y + yy
    mrow = _mod_row(mod_ref, pl.program_id(0), n_batch, tiles_per_batch)
    g1 = mrow[:, 2 * d:3 * d]
    sh2 = mrow[:, 3 * d:4 * d]
    sc2 = mrow[:, 4 * d:5 * d]
    x1 = x_ref[...] + g1 * y
    x1_ref[...] = x1
    h = (_rmsnorm_rows(x1, g_ref[...]) * (1.0 + sc2) + sh2).astype(BF16)
    h2_ref[...] = h
    for hp in range(2 * PEER_HEADS):
        qh = _dot(h, wq_ref[:, hp * LANES:(hp + 1) * LANES]).astype(BF16)
        st_ref[0, hp] = _dot_nt(keys_ref[hp], qh)


def _out_proj(o_parts, w_parts, x, mod, g, wq, keys, n_batch, tiles_per_batch):
    t, d = x.shape
    nt = t // TOK_TILE
    n_parts = len(o_parts)
    in_specs = [pl.BlockSpec((TOK_TILE, o.shape[1]), lambda i: (i, 0)) for o in o_parts]
    in_specs += [pl.BlockSpec(w.shape, lambda i: (0, 0)) for w in w_parts]
    in_specs += [
        pl.BlockSpec((TOK_TILE, d), lambda i: (i, 0)),
        pl.BlockSpec(mod.shape, lambda i: (0, 0)),
        pl.BlockSpec((1, d), lambda i: (0, 0)),
        pl.BlockSpec(wq.shape, lambda i: (0, 0)),
        pl.BlockSpec(keys.shape, lambda i: (0, 0, 0)),
    ]
    kern = functools.partial(_out_kernel, n_parts=n_parts, n_batch=n_batch, tiles_per_batch=tiles_per_batch)
    return pl.pallas_call(
        kern,
        grid=(nt,),
        in_specs=in_specs,
        out_specs=[
            pl.BlockSpec((TOK_TILE, d), lambda i: (i, 0)),
            pl.BlockSpec((TOK_TILE, d), lambda i: (i, 0)),
            pl.BlockSpec((1, 2 * PEER_HEADS, PEER_NKEYS, TOK_TILE), lambda i: (i, 0, 0, 0)),
        ],
        out_shape=[
            jax.ShapeDtypeStruct((t, d), F32),
            jax.ShapeDtypeStruct((t, d), BF16),
            jax.ShapeDtypeStruct((nt, 2 * PEER_HEADS, PEER_NKEYS, TOK_TILE), F32),
        ],
        compiler_params=_cparams(("arbitrary",)),
        name="out_proj",
    )(*o_parts, *w_parts, x, mod, g.reshape(1, d), wq, keys)


def _rank_topk(a, sorted_ref):
    iota = lax.broadcasted_iota(jnp.int32, a.shape, 0)

    def body(r, carry):
        cur, rank = carry
        m = jnp.max(cur, axis=0, keepdims=True)
        first = jnp.min(jnp.where(cur == m, iota, PEER_NKEYS), axis=0, keepdims=True)
        sel = iota == first
        sorted_ref[pl.ds(r, 1), :] = m
        return jnp.where(sel, -jnp.inf, cur), jnp.where(sel, r.astype(F32), rank)

    _, rank = lax.fori_loop(0, PEER_TOPK, body, (a, jnp.full(a.shape, NOT_TOPK, F32)))
    return rank


def _rank_pair_distinct(a_ref, b_ref, sa_ref, sb_ref):
    shape = (1, a_ref.shape[1])

    def body(r, carry):
        pa, pb = carry
        a = a_ref[...]
        b = b_ref[...]
        ma = jnp.max(jnp.where(a < pa, a, -jnp.inf), axis=0, keepdims=True)
        mb = jnp.max(jnp.where(b < pb, b, -jnp.inf), axis=0, keepdims=True)
        sa_ref[pl.ds(r, 1), :] = ma
        sb_ref[pl.ds(r, 1), :] = mb
        return ma, mb

    inf = jnp.full(shape, jnp.inf, F32)
    last_a, last_b = lax.fori_loop(0, PEER_TOPK, body, (inf, inf))
    a = a_ref[...]
    b = b_ref[...]
    ra = jnp.zeros(a.shape, F32)
    rb = jnp.zeros(b.shape, F32)
    for r in range(PEER_TOPK):
        ra = ra + jnp.where(sa_ref[r:r + 1, :] > a, 1.0, 0.0)
        rb = rb + jnp.where(sb_ref[r:r + 1, :] > b, 1.0, 0.0)
    cnt_a = jnp.sum(jnp.where(a >= last_a, 1.0, 0.0), axis=0, keepdims=True)
    cnt_b = jnp.sum(jnp.where(b >= last_b, 1.0, 0.0), axis=0, keepdims=True)
    return ra, rb, cnt_a, cnt_b


def _route_kernel(st_ref, r2_ref, eb_ref, nn_ref, ea_ref, sa_ref, sb_ref, ra_ref, rb_ref):
    k = PEER_TOPK
    kf = float(k)
    a_ref = st_ref.at[0, 0]
    b_ref = st_ref.at[0, 1]
    ra, rb, cnt_a, cnt_b = _rank_pair_distinct(a_ref, b_ref, sa_ref, sb_ref)
    ra_ref[...] = ra
    rb_ref[...] = rb
    miscount = jnp.max(jnp.maximum(jnp.abs(cnt_a - kf), jnp.abs(cnt_b - kf)))

    @pl.when(miscount > 0.0)
    def _():
        ra_ref[...] = _rank_topk(a_ref[...], sa_ref)
        rb_ref[...] = _rank_topk(b_ref[...], sb_ref)

    a = a_ref[...]
    b = b_ref[...]
    ra = ra_ref[...]
    rb = rb_ref[...]
    sa = sa_ref[...]
    sb = sb_ref[...]
    riota = lax.broadcasted_iota(jnp.int32, sa.shape, 0)

    def walk(_, n):
        sbn = jnp.full(sa.shape, -jnp.inf, F32)
        for r2 in range(k):
            sbn = jnp.where(n == r2, sb[r2:r2 + 1, :], sbn)
        f = sa + sbn
        m = jnp.max(f, axis=0, keepdims=True)
        first = jnp.min(jnp.where(f == m, riota, k), axis=0, keepdims=True)
        return n + (riota == first).astype(jnp.int32)

    n = lax.fori_loop(0, k, walk, jnp.zeros(sa.shape, jnp.int32))
    ea_s = jnp.exp(sa - sa[0:1, :])
    eb_s = jnp.exp(sb - sb[0:1, :])
    z = jnp.zeros((1, sa.shape[1]), F32)
    for r2 in range(k):
        z = z + eb_s[r2:r2 + 1, :] * jnp.sum(jnp.where(n > r2, ea_s, 0.0), axis=0, keepdims=True)
    nf = n.astype(F32)
    nn = jnp.zeros(a.shape, F32)
    for r1 in range(k):
        nn = jnp.where(ra == float(r1), nf[r1:r1 + 1, :], nn)
    r2_ref[0, 0] = rb
    eb_ref[0, 0] = jnp.exp(b - sb[0:1, :])
    nn_ref[0, 0, :PEER_NKEYS, :] = nn
    nn_ref[0, 0, PEER_NKEYS:, :] = jnp.zeros((NN_PAD_ROWS, a.shape[1]), F32)
    ea_ref[0, 0] = jnp.where(ra < kf, jnp.exp(a - sa[0:1, :]), 0.0) / z


def _route(st):
    nt = st.shape[0]
    tab = jax.ShapeDtypeStruct((nt, PEER_HEADS, PEER_NKEYS, TOK_TILE), F32)
    spec = pl.BlockSpec((1, 1, PEER_NKEYS, TOK_TILE), lambda i, h: (i, h, 0, 0))
    nn_rows = PEER_NKEYS + NN_PAD_ROWS
    nn_tab = jax.ShapeDtypeStruct((nt, PEER_HEADS, nn_rows, TOK_TILE), F32)
    nn_spec = pl.BlockSpec((1, 1, nn_rows, TOK_TILE), lambda i, h: (i, h, 0, 0))
    return pl.pallas_call(
        _route_kernel,
        grid=(nt, PEER_HEADS),
        in_specs=[pl.BlockSpec((1, 2, PEER_NKEYS, TOK_TILE), lambda i, h: (i, h, 0, 0))],
        out_specs=[spec, spec, nn_spec, spec],
        out_shape=[tab, tab, nn_tab, tab],
        scratch_shapes=[pltpu.VMEM((PEER_TOPK, TOK_TILE), F32), pltpu.VMEM((PEER_TOPK, TOK_TILE), F32),
                        pltpu.VMEM((PEER_NKEYS, TOK_TILE), F32), pltpu.VMEM((PEER_NKEYS, TOK_TILE), F32)],
        compiler_params=_cparams(("arbitrary", "arbitrary")),
        name="peer_route",
    )(st)


def _peer_kernel(h_ref, u_ref, un_ref, vtlo_ref, vthi_ref, r2_ref, eb_ref, nn_ref, ea_ref, x_ref, mod_ref,
                 o_ref, acc_ref, sta_ref, stb_ref, ga_ref, *, n_batch, tiles_per_batch):
    j = pl.program_id(1)
    n = pl.num_programs(1) - 1
    d = D_MODEL
    half = PEER_STEP // 2
    chunks = PEER_STEP // PEER_CHUNK
    i1_per_step = PEER_STEP // PEER_NKEYS

    @pl.when(j == 0)
    def _():
        acc_ref[...] = jnp.zeros(acc_ref.shape, F32)
        ga_ref[half:, :] = jnp.zeros((half, ga_ref.shape[1]), BF16)
        sta_ref[...] = _dot_nt(u_ref[0:PEER_CHUNK, :], h_ref[...])

    acc_ref[...] += _dot(vthi_ref[...], ga_ref[half:, :])
    st_refs = (sta_ref, stb_ref)
    for c in range(chunks):
        nxt = st_refs[(c + 1) % 2]
        if c + 1 < chunks:
            nxt[...] = _dot_nt(u_ref[(c + 1) * PEER_CHUNK:(c + 2) * PEER_CHUNK, :], h_ref[...])
        else:
            nxt[...] = _dot_nt(un_ref[...], h_ref[...])
        cur = st_refs[c % 2]
        for q in range(PEER_CHUNK // PEER_NKEYS):
            i1 = j * i1_per_step + c * (PEER_CHUNK // PEER_NKEYS) + q
            i1_gate = jnp.where(j < n, i1, PEER_NKEYS)
            i1_val = jnp.minimum(i1, PEER_NKEYS - 1)
            n_rows = [nn_ref[0, hd, pl.ds(i1_gate, 1), :] for hd in range(PEER_HEADS)]
            ea_rows = [ea_ref[0, hd, pl.ds(i1_val, 1), :] for hd in range(PEER_HEADS)]
            for lc in range(TOK_TILE // LANES):
                lanes = slice(lc * LANES, (lc + 1) * LANES)
                n_b = [jnp.broadcast_to(r[:, lanes], (PEER_SUB, LANES)) for r in n_rows]
                ea_b = [jnp.broadcast_to(r[:, lanes], (PEER_SUB, LANES)) for r in ea_rows]
                for rb in range(PEER_NKEYS // (2 * PEER_SUB)):
                    pair = []
                    for s in range(2):
                        k0 = (2 * rb + s) * PEER_SUB
                        keys = slice(k0, k0 + PEER_SUB)
                        st = cur[q * PEER_NKEYS + k0:q * PEER_NKEYS + k0 + PEER_SUB, lanes]
                        act = 0.5 * st * (1.0 + lax.erf(st * (1.0 / math.sqrt(2.0))))
                        gate = None
                        for hd in range(PEER_HEADS):
                            term = jnp.where(r2_ref[0, hd, keys, lanes] < n_b[hd],
                                             ea_b[hd] * eb_ref[0, hd, keys, lanes], 0.0)
                            gate = term if gate is None else gate + term
                        pair.append(gate * act)
                    r0 = c * PEER_CHUNK + q * PEER_NKEYS + 2 * rb * PEER_SUB
                    ga_ref[r0:r0 + 2 * PEER_SUB, lanes] = jnp.concatenate(pair, axis=0).astype(BF16)
        if c == chunks // 2 - 1:
            acc_ref[...] += _dot(vtlo_ref[...], ga_ref[:half, :])

    @pl.when(j == n)
    def _():
        mrow = _mod_row(mod_ref, pl.program_id(0), n_batch, tiles_per_batch)
        g2 = mrow[:, 5 * d:6 * d]
        o_ref[...] = x_ref[...] + g2 * acc_ref[...].T


def _peer(h2, u, vt, tables, x1, mod, n_batch, tiles_per_batch):
    t, d = x1.shape
    n = u.shape[0] // PEER_STEP
    chunks = PEER_STEP // PEER_CHUNK
    half = PEER_STEP // 2
    tab_spec = pl.BlockSpec((1, PEER_HEADS, PEER_NKEYS, TOK_TILE), lambda i, j: (i, 0, 0, 0))
    nn_spec = pl.BlockSpec((1, PEER_HEADS, PEER_NKEYS + NN_PAD_ROWS, TOK_TILE), lambda i, j: (i, 0, 0, 0))
    kern = functools.partial(_peer_kernel, n_batch=n_batch, tiles_per_batch=tiles_per_batch)
    return pl.pallas_call(
        kern,
        grid=(t // TOK_TILE, n + 1),
        in_specs=[
            pl.BlockSpec((TOK_TILE, d), lambda i, j: (i, 0)),
            pl.BlockSpec((PEER_STEP, d), lambda i, j: (jnp.minimum(j, n - 1), 0)),
            pl.BlockSpec((PEER_CHUNK, d), lambda i, j: (jnp.minimum((j + 1) * chunks, n * chunks - 1), 0)),
            pl.BlockSpec((d, half), lambda i, j: (0, jnp.minimum(2 * j, 2 * n - 1))),
            pl.BlockSpec((d, half), lambda i, j: (0, jnp.maximum(2 * j - 1, 0))),
            tab_spec, tab_spec, nn_spec, tab_spec,
            pl.BlockSpec((TOK_TILE, d), lambda i, j: (i, 0)),
            pl.BlockSpec(mod.shape, lambda i, j: (0, 0)),
        ],
        out_specs=pl.BlockSpec((TOK_TILE, d), lambda i, j: (i, 0)),
        out_shape=jax.ShapeDtypeStruct((t, d), F32),
        scratch_shapes=[pltpu.VMEM((d, TOK_TILE), F32),
                        pltpu.VMEM((PEER_CHUNK, TOK_TILE), F32), pltpu.VMEM((PEER_CHUNK, TOK_TILE), F32),
                        pltpu.VMEM((PEER_STEP, TOK_TILE), BF16)],
        compiler_params=_cparams(("arbitrary", "arbitrary")),
        name="peer_experts",
    )(h2, u, u, vt, vt, *tables, x1, mod)


def _final_kernel(x_ref, g_ref, o_ref):
    o_ref[...] = _rmsnorm_rows(x_ref[...], g_ref[...])


def _final_norm(x, g, first_tile, n_tiles):
    d = x.shape[1]
    return pl.pallas_call(
        _final_kernel,
        grid=(n_tiles,),
        in_specs=[pl.BlockSpec((TOK_TILE, d), lambda i: (first_tile + i, 0)),
                  pl.BlockSpec((1, d), lambda i: (0, 0))],
        out_specs=pl.BlockSpec((TOK_TILE, d), lambda i: (i, 0)),
        out_shape=jax.ShapeDtypeStruct((n_tiles * TOK_TILE, d), F32),
        compiler_params=_cparams(("arbitrary",)),
        name="final_norm",
    )(x, g.reshape(1, d))


def _rope_tables(n_batch, seq):
    pos = jnp.arange(seq)
    row = (pos // GRID_W).astype(F32)
    col = (pos % GRID_W).astype(F32)
    half = HEAD_DIM // 2
    inv = ROPE_THETA ** (-jnp.arange(0, half, 2, dtype=F32) / half)
    ar = row[:, None] * inv
    ac = col[:, None] * inv
    ang = jnp.concatenate([ar, ar, ac, ac], axis=-1)
    sign = jnp.where((jnp.arange(HEAD_DIM) % 32) < 16, -1.0, 1.0).astype(F32)
    cos = jnp.tile(jnp.cos(ang), (n_batch, LANES // HEAD_DIM))
    sin = jnp.tile(jnp.sin(ang) * sign, (n_batch, LANES // HEAD_DIM))
    n_ctx = n_batch * CTX_LEN
    cos = jnp.concatenate([jnp.ones((n_ctx, LANES), F32), cos], axis=0)
    sin = jnp.concatenate([jnp.zeros((n_ctx, LANES), F32), sin], axis=0)
    return cos, sin


def _even_w_in(w):
    a = NA_HEADS * HEAD_DIM
    scale = jnp.ones((w.shape[1],), F32)
    scale = scale.at[0:a].set(HEAD_DIM ** -0.5)
    scale = scale.at[3 * a:3 * a + 512].set(HEAD_DIM ** -0.5)
    return (w * scale).astype(BF16)


def _odd_w_in(w):
    nq = SWA_Q_HEADS * HEAD_DIM
    nkv = SWA_KV_HEADS * HEAD_DIM
    d = w.shape[0]
    q = w[:, :nq] * (HEAD_DIM ** -0.5)
    k = w[:, nq:nq + nkv].reshape(d, SWA_KV_HEADS, 1, HEAD_DIM)
    v = w[:, nq + nkv:].reshape(d, SWA_KV_HEADS, 1, HEAD_DIM)
    k2 = jnp.broadcast_to(k, (d, SWA_KV_HEADS, 2, HEAD_DIM)).reshape(d, 2 * nkv)
    v2 = jnp.broadcast_to(v, (d, SWA_KV_HEADS, 2, HEAD_DIM)).reshape(d, 2 * nkv)
    return jnp.concatenate([q, k2, v2], axis=1).astype(BF16)


def kernel(x, c, ctx, c_ctx, w_mod, b_mod, norm1_g, norm2_g, w_in_even, w_out_even, na_rpb, diff_lambda,
           diff_subln_g, w_in_odd, w_out_odd, swa_sink, peer_wq, peer_keys, peer_u, peer_v, final_g):
    n_batch, seq, d = x.shape
    depth = w_mod.shape[0]
    assert d == D_MODEL and ctx.shape[1] == CTX_LEN and (n_batch * CTX_LEN) == TOK_TILE
    assert seq % TOK_TILE == 0 and seq // ATT_BLOCK >= NA_WIN_BLOCKS
    tiles_per_batch = seq // TOK_TILE
    lat_blocks = seq // ATT_BLOCK
    rows = seq // GRID_W

    xt = jnp.concatenate([ctx.reshape(n_batch * CTX_LEN, d), x.reshape(n_batch * seq, d)], axis=0)
    cvec = jnp.concatenate([c, c_ctx[None, :],
                            jnp.zeros((MOD_ROWS - n_batch - 1, d), F32)], axis=0)
    mod = _modulation(cvec, w_mod, b_mod)
    cos, sin = _rope_tables(n_batch, seq)

    for i in range(depth):
        j = i // 2
        if i % 2 == 0:
            lam_init = 0.8 - 0.6 * math.exp(-0.3 * i)
            qkv = _in_proj(xt, mod[i], norm1_g[i], _even_w_in(w_in_even[j]), cos, sin,
                           (False, False, False, True, True, False), n_batch, tiles_per_batch)
            oa = _na_attention(qkv, _na_table(na_rpb[j], rows), n_batch, lat_blocks)
            ob = _diff_attention(qkv, diff_lambda[j], diff_subln_g[j], lam_init, n_batch, seq)
            w_out = w_out_even[j].astype(BF16)
            o_parts, w_parts = [oa, ob], [w_out[:512], w_out[512:]]
        else:
            qkv = _in_proj(xt, mod[i], norm1_g[i], _odd_w_in(w_in_odd[j]), cos, sin,
                           (True, True, True, False), n_batch, tiles_per_batch)
            o = _swa_attention(qkv, swa_sink[j], n_batch, lat_blocks, seq)
            o_parts, w_parts = [o], [w_out_odd[j].astype(BF16)]
        keys = peer_keys[i].reshape(2 * PEER_HEADS, PEER_NKEYS, PEER_NKEYS).astype(BF16)
        x1, h2, st = _out_proj(o_parts, w_parts, xt, mod[i], norm2_g[i], peer_wq[i].astype(BF16), keys,
                               n_batch, tiles_per_batch)
        tables = _route(st)
        xt = _peer(h2, peer_u[i].astype(BF16), peer_v[i].T.astype(BF16), tables, x1, mod[i],
                   n_batch, tiles_per_batch)

    out = _final_norm(xt, final_g, n_batch * CTX_LEN // TOK_TILE, n_batch * seq // TOK_TILE)
    return out.reshape(n_batch, seq, d)
```

```python
import functools
import math

import jax
import jax.numpy as jnp
import numpy as np
from jax import lax
from jax.experimental import pallas as pl
from jax.experimental.pallas import tpu as pltpu

F32 = jnp.float32
BF16 = jnp.bfloat16

D_MODEL = 1024
CTX_LEN = 256
GRID_W = 64
HEAD_DIM = 64
ROPE_THETA = 10000.0
EPS = 1e-6
NEG_INF = -1e30

NA_HEADS = 8
NA_WIN_H = 8
NA_WIN_W = 16
DIFF_HEADS = 4
SWA_Q_HEADS = 16
SWA_KV_HEADS = 4
SWA_WINDOW = 128
PEER_HEADS = 8
PEER_NKEYS = 128
PEER_TOPK = 16

LANES = 128
TOK_TILE = 512
ATT_BLOCK = 256
NA_ROWS_PER_BLOCK = ATT_BLOCK // GRID_W
NA_WIN_BLOCKS = 3
DIFF_TQ = 512
DIFF_TK = 512
PEER_STEP = 1024
PEER_CHUNK = 256
PEER_SUB = 8
NOT_TOPK = 999.0
MOD_ROWS = 8
VMEM_LIMIT = 56 * 1024 * 1024


def _cparams(sem):
    return pltpu.CompilerParams(dimension_semantics=sem, vmem_limit_bytes=VMEM_LIMIT)


def _dot(a, b):
    return jnp.dot(a, b, preferred_element_type=F32)


def _dot_nt(a, b):
    return lax.dot_general(a, b, (((1,), (1,)), ((), ())), preferred_element_type=F32)


def _rep_lanes(v, n):
    return v if n == 1 else jnp.concatenate([v] * n, axis=1)


def _mod_row(mod_ref, i, n_batch, tiles_per_batch):
    r = jnp.where(i == 0, n_batch, jnp.maximum(i - 1, 0) // tiles_per_batch)
    return mod_ref[pl.ds(r, 1), :]


def _rmsnorm_rows(x, g):
    ms = jnp.mean(x * x, axis=-1, keepdims=True)
    return x * lax.rsqrt(ms + EPS) * g


def _mod_kernel(cv_ref, w_ref, b_ref, o_ref):
    cv = cv_ref[...]
    s = cv * (1.0 / (1.0 + jnp.exp(-cv)))
    o_ref[0] = jnp.dot(s, w_ref[0], preferred_element_type=F32,
                       precision=lax.Precision.HIGHEST) + b_ref[0]


def _modulation(cvec, w_mod, b_mod):
    depth, d, n = w_mod.shape
    tn = 1536
    return pl.pallas_call(
        _mod_kernel,
        grid=(depth, n // tn),
        in_specs=[
            pl.BlockSpec((MOD_ROWS, d), lambda l, j: (0, 0)),
            pl.BlockSpec((1, d, tn), lambda l, j: (l, 0, j)),
            pl.BlockSpec((1, 1, tn), lambda l, j: (l, 0, j)),
        ],
        out_specs=pl.BlockSpec((1, MOD_ROWS, tn), lambda l, j: (l, 0, j)),
        out_shape=jax.ShapeDtypeStruct((depth, MOD_ROWS, n), F32),
        compiler_params=_cparams(("arbitrary", "arbitrary")),
        name="modulation",
    )(cvec, w_mod, b_mod.reshape(depth, 1, n))


def _in_kernel(x_ref, mod_ref, g_ref, w_ref, cos_ref, sin_ref, o_ref, *, n_batch, tiles_per_batch,
               rope_chunks):
    d = D_MODEL
    mrow = _mod_row(mod_ref, pl.program_id(0), n_batch, tiles_per_batch)
    sh = mrow[:, 0:d]
    sc = mrow[:, d:2 * d]
    h = (_rmsnorm_rows(x_ref[...], g_ref[...]) * (1.0 + sc) + sh).astype(BF16)
    lane = lax.broadcasted_iota(jnp.int32, (1, LANES), 1)
    first = (lane % 32) < 16
    for c, rope in enumerate(rope_chunks):
        cols = slice(c * 512, (c + 1) * 512)
        p = _dot(h, w_ref[:, cols])
        if rope:
            cos = cos_ref[...]
            sin = sin_ref[...]
            parts = []
            for s in range(512 // LANES):
                ps = p[:, s * LANES:(s + 1) * LANES]
                rot = jnp.where(first, pltpu.roll(ps, LANES - 16, 1), pltpu.roll(ps, 16, 1))
                parts.append(ps * cos + rot * sin)
            p = jnp.concatenate(parts, axis=1)
        o_ref[:, cols] = p.astype(BF16)


def _in_proj(x, mod, g, w, cos, sin, rope_chunks, n_batch, tiles_per_batch):
    t, d = x.shape
    n = w.shape[1]
    kern = functools.partial(_in_kernel, n_batch=n_batch, tiles_per_batch=tiles_per_batch,
                             rope_chunks=rope_chunks)
    return pl.pallas_call(
        kern,
        grid=(t // TOK_TILE,),
        in_specs=[
            pl.BlockSpec((TOK_TILE, d), lambda i: (i, 0)),
            pl.BlockSpec(mod.shape, lambda i: (0, 0)),
            pl.BlockSpec((1, d), lambda i: (0, 0)),
            pl.BlockSpec((d, n), lambda i: (0, 0)),
            pl.BlockSpec((TOK_TILE, LANES), lambda i: (i, 0)),
            pl.BlockSpec((TOK_TILE, LANES), lambda i: (i, 0)),
        ],
        out_specs=pl.BlockSpec((TOK_TILE, n), lambda i: (i, 0)),
        out_shape=jax.ShapeDtypeStruct((t, n), BF16),
        compiler_params=_cparams(("arbitrary",)),
        name="in_proj",
    )(x, mod, g.reshape(1, d), w, cos, sin)


def _half_masks():
    lane = lax.broadcasted_iota(jnp.int32, (1, LANES), 1)
    lo = lane < HEAD_DIM
    return lo, jnp.logical_not(lo)


def _softmax_pv(score_parts, v_parts, extra_logit=None):
    m = None
    for s in score_parts:
        mm = jnp.max(s, axis=-1, keepdims=True)
        m = mm if m is None else jnp.maximum(m, mm)
    if extra_logit is not None:
        m = jnp.maximum(m, extra_logit)
    l = None
    o = None
    for s, v in zip(score_parts, v_parts):
        p = jnp.exp(s - m)
        ll = jnp.sum(p, axis=-1, keepdims=True)
        oo = _dot(p.astype(BF16), v)
        l = ll if l is None else l + ll
        o = oo if o is None else o + oo
    if extra_logit is not None:
        l = l + jnp.exp(extra_logit - m)
    return o / l


def _na_kernel(q_ref, k0_ref, k1_ref, k2_ref, v0_ref, v1_ref, v2_ref, kc_ref, vc_ref, tab_ref, o_ref):
    lo, hi = _half_masks()
    k_refs = (k0_ref, k1_ref, k2_ref)
    v_refs = (v0_ref, v1_ref, v2_ref)
    for s in range(NA_HEADS // 2):
        sl = slice(s * LANES, (s + 1) * LANES)
        qs = q_ref[:, sl]
        outs = []
        for half, msk in enumerate((lo, hi)):
            qm = jnp.where(msk, qs, jnp.zeros_like(qs))
            head = 2 * s + half
            scores = [_dot_nt(qm, k_refs[w][:, sl])
                      + tab_ref[0, head, :, w * ATT_BLOCK:(w + 1) * ATT_BLOCK]
                      for w in range(NA_WIN_BLOCKS)]
            scores.append(_dot_nt(qm, kc_ref[:, sl]))
            vals = [v_refs[w][:, sl] for w in range(NA_WIN_BLOCKS)] + [vc_ref[:, sl]]
            outs.append(_softmax_pv(scores, vals))
        o_ref[:, sl] = jnp.where(lo, outs[0], outs[1]).astype(BF16)


def _na_table(rpb, rows):
    win_rows = NA_WIN_BLOCKS * NA_ROWS_PER_BLOCK
    qc = np.arange(GRID_W)
    cs = np.clip(qc - NA_WIN_W // 2, 0, GRID_W - NA_WIN_W)
    kc = np.arange(GRID_W)
    valid_c = (kc[None, :] >= cs[:, None]) & (kc[None, :] < cs[:, None] + NA_WIN_W)
    dc = kc[None, :] - qc[:, None] + (NA_WIN_W - 1)
    sel_c = (valid_c[:, :, None] & (dc[:, :, None] == np.arange(2 * NA_WIN_W - 1))).astype(np.float32)

    def row_geometry(r0, base_row):
        r = r0 + np.arange(NA_ROWS_PER_BLOCK)
        rs = np.clip(r - NA_WIN_H // 2, 0, rows - NA_WIN_H)
        kr = base_row + np.arange(win_rows)
        valid_r = (kr[None, :] >= rs[:, None]) & (kr[None, :] < rs[:, None] + NA_WIN_H)
        dr = kr[None, :] - r[:, None] + (NA_WIN_H - 1)
        sel_r = (valid_r[:, :, None] & (dr[:, :, None] == np.arange(2 * NA_WIN_H - 1))).astype(np.float32)
        return valid_r, sel_r

    geoms = [row_geometry(0, 0), row_geometry(NA_ROWS_PER_BLOCK, 0),
             row_geometry(rows - NA_ROWS_PER_BLOCK, rows - win_rows)]
    sel_r = np.stack([np.zeros_like(geoms[0][1])] + [g[1] for g in geoms])
    valid = np.stack([np.zeros_like(geoms[0][0])] + [g[0] for g in geoms])
    valid = valid[:, :, None, :, None] & valid_c[None, None, :, None, :]
    mask = np.where(valid, 0.0, NEG_INF).astype(np.float32)
    bias = jnp.einsum("hrc,gqkr,xyc->ghqxky", rpb.astype(F32), sel_r, sel_c,
                      precision=lax.Precision.HIGHEST)
    table = bias + mask[:, None]
    return table.reshape(4, rpb.shape[0], ATT_BLOCK, NA_WIN_BLOCKS * ATT_BLOCK)


def _na_attention(qkv, table, n_batch, lat_blocks):
    t = qkv.shape[0]
    nb, lb = n_batch, lat_blocks

    def q_row(b, j):
        return jnp.where(j == 0, b, nb + b * lb + jnp.maximum(j - 1, 0))

    def win_row(w):
        def f(b, j):
            return nb + b * lb + jnp.clip(j - 2, 0, lb - NA_WIN_BLOCKS) + w
        return f

    def case(b, j):
        return jnp.where(j == 0, 0, jnp.where(j == 1, 1, jnp.where(j == lb, 3, 2)))

    blk = (ATT_BLOCK, 512)
    in_specs = [pl.BlockSpec(blk, lambda b, j: (q_row(b, j), 0))]
    in_specs += [pl.BlockSpec(blk, lambda b, j, w=w: (win_row(w)(b, j), 1)) for w in range(NA_WIN_BLOCKS)]
    in_specs += [pl.BlockSpec(blk, lambda b, j, w=w: (win_row(w)(b, j), 2)) for w in range(NA_WIN_BLOCKS)]
    in_specs += [pl.BlockSpec(blk, lambda b, j: (b, 1)), pl.BlockSpec(blk, lambda b, j: (b, 2))]
    in_specs += [pl.BlockSpec((1,) + table.shape[1:], lambda b, j: (case(b, j), 0, 0, 0))]
    return pl.pallas_call(
        _na_kernel,
        grid=(nb, lb + 1),
        in_specs=in_specs,
        out_specs=pl.BlockSpec(blk, lambda b, j: (q_row(b, j), 0)),
        out_shape=jax.ShapeDtypeStruct((t, 512), BF16),
        compiler_params=_cparams(("arbitrary", "arbitrary")),
        name="na_attention",
    )(qkv, qkv, qkv, qkv, qkv, qkv, qkv, qkv, qkv, table)


def _diff_kernel(*refs, latent, lam_init):
    if latent:
        (q_ref, kc_ref, vc_ref, k0_ref, v0_ref, k1_ref, v1_ref, lam_ref, g_ref, o_ref,
         m_scr, l_scr, acc_scr) = refs
        j = pl.program_id(1)
        t = pl.program_id(2)
        last = pl.num_programs(2) - 1
    else:
        q_ref, kc_ref, vc_ref, lam_ref, g_ref, _, o_ref, m_scr, l_scr, acc_scr = refs
    lo, hi = _half_masks()

    def process(kv_parts):
        for h in range(DIFF_HEADS):
            sl = slice(h * LANES, (h + 1) * LANES)
            qs = q_ref[:, sl]
            for mp, msk in enumerate((lo, hi)):
                idx = 2 * h + mp
                qm = jnp.where(msk, qs, jnp.zeros_like(qs))
                scores = [_dot_nt(qm, kr[:, sl]) for kr, _ in kv_parts]
                m_prev = m_scr[idx]
                m_new = m_prev
                for s in scores:
                    m_new = jnp.maximum(m_new, jnp.max(s, axis=-1, keepdims=True))
                alpha = jnp.exp(m_prev - m_new)
                l_new = alpha * l_scr[idx]
                acc_new = alpha * acc_scr[idx]
                for s, (_, vr) in zip(scores, kv_parts):
                    p = jnp.exp(s - _rep_lanes(m_new, s.shape[1] // LANES))
                    l_new = l_new + jnp.sum(p, axis=-1, keepdims=True)
                    acc_new = acc_new + _dot(p.astype(BF16), vr[:, sl])
                l_scr[idx] = l_new
                acc_scr[idx] = acc_new
                m_scr[idx] = m_new

    def init():
        m_scr[...] = jnp.full(m_scr.shape, NEG_INF, F32)
        l_scr[...] = jnp.zeros(l_scr.shape, F32)
        acc_scr[...] = jnp.zeros(acc_scr.shape, F32)

    def finalize():
        lp = lam_ref[...]
        lam = (jnp.exp(jnp.sum(lp[0:1] * lp[1:2], axis=-1, keepdims=True))
               - jnp.exp(jnp.sum(lp[2:3] * lp[3:4], axis=-1, keepdims=True)) + lam_init)
        g = g_ref[...]
        for h in range(DIFF_HEADS):
            o = acc_scr[2 * h] / l_scr[2 * h] - lam * (acc_scr[2 * h + 1] / l_scr[2 * h + 1])
            y = _rmsnorm_rows(o, g) * (1.0 - lam_init)
            o_ref[:, h * LANES:(h + 1) * LANES] = y.astype(BF16)

    if latent:
        @pl.when(jnp.logical_and(jnp.logical_and(j == 0, t == 0), pl.program_id(0) == 0))
        def _():
            o_ref[...] = jnp.zeros(o_ref.shape, o_ref.dtype)

        @pl.when(jnp.logical_and(j > 0, t == 0))
        def _():
            init()
            process([(kc_ref, vc_ref)])

        @pl.when(j > 0)
        def _():
            process([(k0_ref, v0_ref), (k1_ref, v1_ref)])

        @pl.when(jnp.logical_and(j > 0, t == last))
        def _():
            finalize()
    else:
        init()
        process([(kc_ref, vc_ref)])
        finalize()


def _diff_attention(qkv, lam_p, subln_g, lam_init, n_batch, seq):
    t = qkv.shape[0]
    nb = n_batch
    nq = seq // DIFF_TQ
    nk = seq // DIFF_TK
    ctx_tiles_q = nb * CTX_LEN // DIFF_TQ
    ctx_tiles_k = nb * CTX_LEN // DIFF_TK
    scratch = lambda tq: [pltpu.VMEM((2 * DIFF_HEADS, tq, LANES), F32),
                          pltpu.VMEM((2 * DIFF_HEADS, tq, LANES), F32),
                          pltpu.VMEM((2 * DIFF_HEADS, tq, LANES), F32)]
    small = [pl.BlockSpec(lam_p.shape, lambda *a: (0, 0)), pl.BlockSpec((1, LANES), lambda *a: (0, 0))]
    g2 = subln_g.reshape(1, LANES)
    assert ctx_tiles_q == 1 and nk % 2 == 0

    def q_tile(b, j):
        first = ctx_tiles_q + b * nq
        return jnp.where(j == 0, jnp.where(b == 0, 0, first), first + jnp.maximum(j - 1, 0))

    def kv_spec(part, col):
        return pl.BlockSpec((DIFF_TK, 512), lambda b, j, k: (ctx_tiles_k + b * nk + 2 * k + part, col))

    out = pl.pallas_call(
        functools.partial(_diff_kernel, latent=True, lam_init=lam_init),
        grid=(nb, nq + 1, nk // 2),
        in_specs=[
            pl.BlockSpec((DIFF_TQ, 512), lambda b, j, k: (q_tile(b, j), 3)),
            pl.BlockSpec((CTX_LEN, 512), lambda b, j, k: (b, 4)),
            pl.BlockSpec((CTX_LEN, 512), lambda b, j, k: (b, 5)),
            kv_spec(0, 4), kv_spec(0, 5), kv_spec(1, 4), kv_spec(1, 5),
        ] + small,
        out_specs=pl.BlockSpec((DIFF_TQ, 512), lambda b, j, k: (q_tile(b, j), 0)),
        out_shape=jax.ShapeDtypeStruct((t, 512), BF16),
        scratch_shapes=scratch(DIFF_TQ),
        compiler_params=_cparams(("arbitrary", "arbitrary", "arbitrary")),
        name="diff_attention",
    )(qkv, qkv, qkv, qkv, qkv, qkv, qkv, lam_p, g2)
    return pl.pallas_call(
        functools.partial(_diff_kernel, latent=False, lam_init=lam_init),
        grid=(nb,),
        in_specs=[
            pl.BlockSpec((CTX_LEN, 512), lambda b: (b, 3)),
            pl.BlockSpec((CTX_LEN, 512), lambda b: (b, 4)),
            pl.BlockSpec((CTX_LEN, 512), lambda b: (b, 5)),
        ] + small + [pl.BlockSpec(memory_space=pl.ANY)],
        out_specs=pl.BlockSpec((CTX_LEN, 512), lambda b: (b, 0)),
        out_shape=jax.ShapeDtypeStruct((t, 512), BF16),
        scratch_shapes=scratch(CTX_LEN),
        input_output_aliases={5: 0},
        compiler_params=_cparams(("arbitrary",)),
        name="diff_attention_ctx",
    )(qkv, qkv, qkv, lam_p, g2, out)


def _swa_kernel(sink_ref, q_ref, k0_ref, k1_ref, k2_ref, v0_ref, v1_ref, v2_ref, kc_ref, vc_ref, o_ref,
                *, seq):
    j = pl.program_id(1)
    jl = j - 1
    lo, hi = _half_masks()
    k_refs = (k0_ref, k1_ref, k2_ref)
    v_refs = (v0_ref, v1_ref, v2_ref)
    qpos = jl * ATT_BLOCK + lax.broadcasted_iota(jnp.int32, (ATT_BLOCK, ATT_BLOCK), 0)
    kiota = lax.broadcasted_iota(jnp.int32, (ATT_BLOCK, ATT_BLOCK), 1)
    valid = []
    for w in range(3):
        kpos = (jl - 1 + w) * ATT_BLOCK + kiota
        valid.append((jnp.abs(kpos - qpos) <= SWA_WINDOW) & (kpos >= 0) & (kpos < seq) & (j >= 1))
    group = SWA_Q_HEADS // SWA_KV_HEADS
    for s in range(SWA_Q_HEADS // 2):
        sl = slice(s * LANES, (s + 1) * LANES)
        n = (2 * s) // group
        kvsl = slice(n * LANES, (n + 1) * LANES)
        qs = q_ref[:, sl]
        outs = []
        for half, msk in enumerate((lo, hi)):
            qm = jnp.where(msk, qs, jnp.zeros_like(qs))
            scores = [jnp.where(valid[w], _dot_nt(qm, k_refs[w][:, kvsl]), NEG_INF) for w in range(3)]
            scores.append(_dot_nt(qm, kc_ref[:, kvsl]))
            vals = [v_refs[w][:, kvsl] for w in range(3)] + [vc_ref[:, kvsl]]
            sink = jnp.full((1, 1), sink_ref[2 * s + half], F32)
            outs.append(_softmax_pv(scores, vals, extra_logit=sink))
        o_ref[:, sl] = jnp.where(lo, outs[0], outs[1]).astype(BF16)


def _swa_attention(qkv, sink, n_batch, lat_blocks, seq):
    t = qkv.shape[0]
    nb, lb = n_batch, lat_blocks

    def q_row(b, j):
        return jnp.where(j == 0, b, nb + b * lb + jnp.maximum(j - 1, 0))

    def win_row(w):
        def f(b, j):
            return nb + b * lb + jnp.clip(j - 2 + w, 0, lb - 1)
        return f

    qblk = (ATT_BLOCK, 1024)
    kblk = (ATT_BLOCK, 512)
    in_specs = [pl.BlockSpec(memory_space=pltpu.SMEM),
                pl.BlockSpec(qblk, lambda b, j: (q_row(b, j), 0))]
    in_specs += [pl.BlockSpec(kblk, lambda b, j, w=w: (win_row(w)(b, j), 2)) for w in range(3)]
    in_specs += [pl.BlockSpec(kblk, lambda b, j, w=w: (win_row(w)(b, j), 3)) for w in range(3)]
    in_specs += [pl.BlockSpec(kblk, lambda b, j: (b, 2)), pl.BlockSpec(kblk, lambda b, j: (b, 3))]
    return pl.pallas_call(
        functools.partial(_swa_kernel, seq=seq),
        grid=(nb, lb + 1),
        in_specs=in_specs,
        out_specs=pl.BlockSpec(qblk, lambda b, j: (q_row(b, j), 0)),
        out_shape=jax.ShapeDtypeStruct((t, 1024), BF16),
        compiler_params=_cparams(("arbitrary", "arbitrary")),
        name="swa_attention",
    )(sink, qkv, qkv, qkv, qkv, qkv, qkv, qkv, qkv, qkv)


def _out_kernel(*refs, n_parts, n_batch, tiles_per_batch):
    o_refs = refs[:n_parts]
    w_refs = refs[n_parts:2 * n_parts]
    x_ref, mod_ref, g_ref, wq_ref, keys_ref, x1_ref, h2_ref, st_ref = refs[2 * n_parts:]
    d = D_MODEL
    y = None
    for o_ref, w_ref in zip(o_refs, w_refs):
        yy = _dot(o_ref[...], w_ref[...])
        y = yy if y is None else y + yy
    mrow = _mod_row(mod_ref, pl.program_id(0), n_batch, tiles_per_batch)
    g1 = mrow[:, 2 * d:3 * d]
    sh2 = mrow[:, 3 * d:4 * d]
    sc2 = mrow[:, 4 * d:5 * d]
    x1 = x_ref[...] + g1 * y
    x1_ref[...] = x1
    h = (_rmsnorm_rows(x1, g_ref[...]) * (1.0 + sc2) + sh2).astype(BF16)
    h2_ref[...] = h
    for hp in range(2 * PEER_HEADS):
        qh = _dot(h, wq_ref[:, hp * LANES:(hp + 1) * LANES]).astype(BF16)
        st_ref[0, hp] = _dot_nt(keys_ref[hp], qh)


def _out_proj(o_parts, w_parts, x, mod, g, wq, keys, n_batch, tiles_per_batch):
    t, d = x.shape
    nt = t // TOK_TILE
    n_parts = len(o_parts)
    in_specs = [pl.BlockSpec((TOK_TILE, o.shape[1]), lambda i: (i, 0)) for o in o_parts]
    in_specs += [pl.BlockSpec(w.shape, lambda i: (0, 0)) for w in w_parts]
    in_specs += [
        pl.BlockSpec((TOK_TILE, d), lambda i: (i, 0)),
        pl.BlockSpec(mod.shape, lambda i: (0, 0)),
        pl.BlockSpec((1, d), lambda i: (0, 0)),
        pl.BlockSpec(wq.shape, lambda i: (0, 0)),
        pl.BlockSpec(keys.shape, lambda i: (0, 0, 0)),
    ]
    kern = functools.partial(_out_kernel, n_parts=n_parts, n_batch=n_batch, tiles_per_batch=tiles_per_batch)
    return pl.pallas_call(
        kern,
        grid=(nt,),
        in_specs=in_specs,
        out_specs=[
            pl.BlockSpec((TOK_TILE, d), lambda i: (i, 0)),
            pl.BlockSpec((TOK_TILE, d), lambda i: (i, 0)),
            pl.BlockSpec((1, 2 * PEER_HEADS, PEER_NKEYS, TOK_TILE), lambda i: (i, 0, 0, 0)),
        ],
        out_shape=[
            jax.ShapeDtypeStruct((t, d), F32),
            jax.ShapeDtypeStruct((t, d), BF16),
            jax.ShapeDtypeStruct((nt, 2 * PEER_HEADS, PEER_NKEYS, TOK_TILE), F32),
        ],
        compiler_params=_cparams(("arbitrary",)),
        name="out_proj",
    )(*o_parts, *w_parts, x, mod, g.reshape(1, d), wq, keys)


def _rank_topk(a, sorted_ref):
    iota = lax.broadcasted_iota(jnp.int32, a.shape, 0)

    def body(r, carry):
        cur, rank = carry
        m = jnp.max(cur, axis=0, keepdims=True)
        first = jnp.min(jnp.where(cur == m, iota, PEER_NKEYS), axis=0, keepdims=True)
        sel = iota == first
        sorted_ref[pl.ds(r, 1), :] = m
        return jnp.where(sel, -jnp.inf, cur), jnp.where(sel, jnp.asarray(r, F32), rank)

    _, rank = lax.fori_loop(0, PEER_TOPK, body, (a, jnp.full(a.shape, NOT_TOPK, F32)))
    return rank


def _rank_pair_distinct(a_ref, b_ref, sa_ref, sb_ref):
    shape = (1, a_ref.shape[1])

    def body(r, carry):
        pa, pb = carry
        a = a_ref[...]
        b = b_ref[...]
        ma = jnp.max(jnp.where(a < pa, a, -jnp.inf), axis=0, keepdims=True)
        mb = jnp.max(jnp.where(b < pb, b, -jnp.inf), axis=0, keepdims=True)
        sa_ref[pl.ds(r, 1), :] = ma
        sb_ref[pl.ds(r, 1), :] = mb
        return ma, mb

    inf = jnp.full(shape, jnp.inf, F32)
    last_a, last_b = lax.fori_loop(0, PEER_TOPK, body, (inf, inf))
    a = a_ref[...]
    b = b_ref[...]
    ra = jnp.zeros(a.shape, F32)
    rb = jnp.zeros(b.shape, F32)
    for r in range(PEER_TOPK):
        ra = ra + jnp.where(sa_ref[r:r + 1, :] > a, 1.0, 0.0)
        rb = rb + jnp.where(sb_ref[r:r + 1, :] > b, 1.0, 0.0)
    cnt_a = jnp.sum(jnp.where(a >= last_a, 1.0, 0.0), axis=0, keepdims=True)
    cnt_b = jnp.sum(jnp.where(b >= last_b, 1.0, 0.0), axis=0, keepdims=True)
    return ra, rb, cnt_a, cnt_b


def _route_kernel(st_ref, r2_ref, eb_ref, nn_ref, ea_ref, sa_ref, sb_ref, ra_ref, rb_ref):
    k = PEER_TOPK
    kf = float(k)
    a_ref = st_ref.at[0, 0]
    b_ref = st_ref.at[0, 1]
    ra, rb, cnt_a, cnt_b = _rank_pair_distinct(a_ref, b_ref, sa_ref, sb_ref)
    ra_ref[...] = ra
    rb_ref[...] = rb
    miscount = jnp.max(jnp.maximum(jnp.abs(cnt_a - kf), jnp.abs(cnt_b - kf)))

    @pl.when(miscount > 0.0)
    def _():
        ra_ref[...] = _rank_topk(a_ref[...], sa_ref)
        rb_ref[...] = _rank_topk(b_ref[...], sb_ref)

    a = a_ref[...]
    b = b_ref[...]
    ra = ra_ref[...]
    rb = rb_ref[...]
    sa = sa_ref[...]
    sb = sb_ref[...]
    riota = lax.broadcasted_iota(jnp.int32, sa.shape, 0)

    def walk(_, n):
        sbn = jnp.full(sa.shape, -jnp.inf, F32)
        for r2 in range(k):
            sbn = jnp.where(n == r2, sb[r2:r2 + 1, :], sbn)
        f = sa + sbn
        m = jnp.max(f, axis=0, keepdims=True)
        first = jnp.min(jnp.where(f == m, riota, k), axis=0, keepdims=True)
        return n + (riota == first).astype(jnp.int32)

    n = lax.fori_loop(0, k, walk, jnp.zeros(sa.shape, jnp.int32))
    ea_s = jnp.exp(sa - sa[0:1, :])
    eb_s = jnp.exp(sb - sb[0:1, :])
    z = jnp.zeros((1, sa.shape[1]), F32)
    for r2 in range(k):
        z = z + eb_s[r2:r2 + 1, :] * jnp.sum(jnp.where(n > r2, ea_s, 0.0), axis=0, keepdims=True)
    nf = n.astype(F32)
    nn = jnp.zeros(a.shape, F32)
    for r1 in range(k):
        nn = jnp.where(ra == float(r1), nf[r1:r1 + 1, :], nn)
    r2_ref[0, 0] = rb
    eb_ref[0, 0] = jnp.exp(b - sb[0:1, :])
    nn_ref[0, 0] = nn
    ea_ref[0, 0] = jnp.where(ra < kf, jnp.exp(a - sa[0:1, :]), 0.0) / z


def _route(st):
    nt = st.shape[0]
    tab = jax.ShapeDtypeStruct((nt, PEER_HEADS, PEER_NKEYS, TOK_TILE), F32)
    spec = pl.BlockSpec((1, 1, PEER_NKEYS, TOK_TILE), lambda i, h: (i, h, 0, 0))
    return pl.pallas_call(
        _route_kernel,
        grid=(nt, PEER_HEADS),
        in_specs=[pl.BlockSpec((1, 2, PEER_NKEYS, TOK_TILE), lambda i, h: (i, h, 0, 0))],
        out_specs=[spec, spec, spec, spec],
        out_shape=[tab, tab, tab, tab],
        scratch_shapes=[pltpu.VMEM((PEER_TOPK, TOK_TILE), F32), pltpu.VMEM((PEER_TOPK, TOK_TILE), F32),
                        pltpu.VMEM((PEER_NKEYS, TOK_TILE), F32), pltpu.VMEM((PEER_NKEYS, TOK_TILE), F32)],
        compiler_params=_cparams(("arbitrary", "arbitrary")),
        name="peer_route",
    )(st)


def _peer_kernel(h_ref, u_ref, vtp_ref, r2_ref, eb_ref, nn_ref, ea_ref, x_ref, mod_ref,
                 o_ref, acc_ref, st_ref, ga_ref, *, n_steps, n_batch, tiles_per_batch):
    j = pl.program_id(1)
    d = D_MODEL
    halves = 2
    rows_per_half = d // halves
    pairs_per_half = PEER_STEP // PEER_CHUNK // halves
    i1_per_pair = PEER_CHUNK // PEER_NKEYS

    @pl.when(j == 0)
    def _():
        acc_ref[...] = jnp.zeros(acc_ref.shape, F32)
        ga_ref[...] = jnp.zeros(ga_ref.shape, BF16)

    def second_matmul(read_slot, it):
        if isinstance(it, int):
            rows = slice(it * rows_per_half, (it + 1) * rows_per_half)
        else:
            rows = pl.ds(pl.multiple_of(it * rows_per_half, rows_per_half), rows_per_half)
        acc_ref[rows, :] += _dot(vtp_ref[rows, :], ga_ref[read_slot])

    def step(write_slot, read_slot):
        def body(it, carry):
            second_matmul(read_slot, it)
            starts = []
            for p in range(pairs_per_half):
                e0 = pl.multiple_of((it * pairs_per_half + p) * PEER_CHUNK, PEER_CHUNK)
                starts.append(e0)
                st_ref[p] = _dot_nt(u_ref[pl.ds(e0, PEER_CHUNK), :], h_ref[...])
            for p in range(pairs_per_half):
                for q in range(i1_per_pair):
                    i1 = (j * (PEER_STEP // PEER_NKEYS) + (it * pairs_per_half + p) * i1_per_pair + q)
                    n_rows = [nn_ref[0, hd, pl.ds(i1, 1), :] for hd in range(PEER_HEADS)]
                    ea_rows = [ea_ref[0, hd, pl.ds(i1, 1), :] for hd in range(PEER_HEADS)]
                    for lc in range(TOK_TILE // LANES):
                        lanes = slice(lc * LANES, (lc + 1) * LANES)
                        n_b = [jnp.broadcast_to(r[:, lanes], (PEER_SUB, LANES)) for r in n_rows]
                        ea_b = [jnp.broadcast_to(r[:, lanes], (PEER_SUB, LANES)) for r in ea_rows]
                        for rb in range(PEER_NKEYS // (2 * PEER_SUB)):
                            blocks = []
                            for s in range(2):
                                k0 = (2 * rb + s) * PEER_SUB
                                keys = slice(k0, k0 + PEER_SUB)
                                st = st_ref[p, q * PEER_NKEYS + k0:q * PEER_NKEYS + k0 + PEER_SUB, lanes]
                                act = 0.5 * st * (1.0 + lax.erf(st * (1.0 / math.sqrt(2.0))))
                                gate = None
                                for hd in range(PEER_HEADS):
                                    term = jnp.where(r2_ref[0, hd, keys, lanes] < n_b[hd],
                                                     ea_b[hd] * eb_ref[0, hd, keys, lanes], 0.0)
                                    gate = term if gate is None else gate + term
                                blocks.append(gate * act)
                            r0 = starts[p] + q * PEER_NKEYS + 2 * rb * PEER_SUB
                            ga_ref[write_slot, pl.ds(pl.multiple_of(r0, 2 * PEER_SUB), 2 * PEER_SUB), lanes] = (
                                jnp.concatenate(blocks, axis=0).astype(BF16))
            return carry

        lax.fori_loop(0, halves, body, 0)

    for slot in range(2):
        @pl.when(jnp.logical_and(j < n_steps, j % 2 == slot))
        def _(slot=slot):
            step(slot, 1 - slot)

    @pl.when(j == n_steps)
    def _():
        for it in range(halves):
            second_matmul((n_steps - 1) % 2, it)
        mrow = _mod_row(mod_ref, pl.program_id(0), n_batch, tiles_per_batch)
        g2 = mrow[:, 5 * d:6 * d]
        o_ref[...] = x_ref[...] + g2 * acc_ref[...].T


def _peer(h2, u, vt, tables, x1, mod, n_batch, tiles_per_batch):
    t, d = x1.shape
    n = u.shape[0] // PEER_STEP
    pairs_per_half = PEER_STEP // PEER_CHUNK // 2
    tab_spec = pl.BlockSpec((1, PEER_HEADS, PEER_NKEYS, TOK_TILE), lambda i, j: (i, 0, 0, 0))
    kern = functools.partial(_peer_kernel, n_steps=n, n_batch=n_batch, tiles_per_batch=tiles_per_batch)
    return pl.pallas_call(
        kern,
        grid=(t // TOK_TILE, n + 1),
        in_specs=[
            pl.BlockSpec((TOK_TILE, d), lambda i, j: (i, 0)),
            pl.BlockSpec((PEER_STEP, d), lambda i, j: (jnp.minimum(j, n - 1), 0)),
            pl.BlockSpec((d, PEER_STEP), lambda i, j: (0, jnp.maximum(j - 1, 0))),
            tab_spec, tab_spec, tab_spec, tab_spec,
            pl.BlockSpec((TOK_TILE, d), lambda i, j: (i, 0)),
            pl.BlockSpec(mod.shape, lambda i, j: (0, 0)),
        ],
        out_specs=pl.BlockSpec((TOK_TILE, d), lambda i, j: (i, 0)),
        out_shape=jax.ShapeDtypeStruct((t, d), F32),
        scratch_shapes=[pltpu.VMEM((d, TOK_TILE), F32),
                        pltpu.VMEM((pairs_per_half, PEER_CHUNK, TOK_TILE), F32),
                        pltpu.VMEM((2, PEER_STEP, TOK_TILE), BF16)],
        compiler_params=_cparams(("arbitrary", "arbitrary")),
        name="peer_experts",
    )(h2, u, vt, *tables, x1, mod)


def _final_kernel(x_ref, g_ref, o_ref):
    o_ref[...] = _rmsnorm_rows(x_ref[...], g_ref[...])


def _final_norm(x, g, first_tile, n_tiles):
    d = x.shape[1]
    return pl.pallas_call(
        _final_kernel,
        grid=(n_tiles,),
        in_specs=[pl.BlockSpec((TOK_TILE, d), lambda i: (first_tile + i, 0)),
                  pl.BlockSpec((1, d), lambda i: (0, 0))],
        out_specs=pl.BlockSpec((TOK_TILE, d), lambda i: (i, 0)),
        out_shape=jax.ShapeDtypeStruct((n_tiles * TOK_TILE, d), F32),
        compiler_params=_cparams(("arbitrary",)),
        name="final_norm",
    )(x, g.reshape(1, d))


def _rope_tables(n_batch, seq):
    pos = jnp.arange(seq)
    row = (pos // GRID_W).astype(F32)
    col = (pos % GRID_W).astype(F32)
    half = HEAD_DIM // 2
    inv = ROPE_THETA ** (-jnp.arange(0, half, 2, dtype=F32) / half)
    ar = row[:, None] * inv
    ac = col[:, None] * inv
    ang = jnp.concatenate([ar, ar, ac, ac], axis=-1)
    sign = jnp.where((jnp.arange(HEAD_DIM) % 32) < 16, -1.0, 1.0).astype(F32)
    cos = jnp.tile(jnp.cos(ang), (n_batch, LANES // HEAD_DIM))
    sin = jnp.tile(jnp.sin(ang) * sign, (n_batch, LANES // HEAD_DIM))
    n_ctx = n_batch * CTX_LEN
    cos = jnp.concatenate([jnp.ones((n_ctx, LANES), F32), cos], axis=0)
    sin = jnp.concatenate([jnp.zeros((n_ctx, LANES), F32), sin], axis=0)
    return cos, sin


def _even_w_in(w):
    a = NA_HEADS * HEAD_DIM
    scale = jnp.ones((w.shape[1],), F32)
    scale = scale.at[0:a].set(HEAD_DIM ** -0.5)
    scale = scale.at[3 * a:3 * a + 512].set(HEAD_DIM ** -0.5)
    return (w * scale).astype(BF16)


def _odd_w_in(w):
    nq = SWA_Q_HEADS * HEAD_DIM
    nkv = SWA_KV_HEADS * HEAD_DIM
    d = w.shape[0]
    q = w[:, :nq] * (HEAD_DIM ** -0.5)
    k = w[:, nq:nq + nkv].reshape(d, SWA_KV_HEADS, 1, HEAD_DIM)
    v = w[:, nq + nkv:].reshape(d, SWA_KV_HEADS, 1, HEAD_DIM)
    k2 = jnp.broadcast_to(k, (d, SWA_KV_HEADS, 2, HEAD_DIM)).reshape(d, 2 * nkv)
    v2 = jnp.broadcast_to(v, (d, SWA_KV_HEADS, 2, HEAD_DIM)).reshape(d, 2 * nkv)
    return jnp.concatenate([q, k2, v2], axis=1).astype(BF16)


def kernel(x, c, ctx, c_ctx, w_mod, b_mod, norm1_g, norm2_g, w_in_even, w_out_even, na_rpb, diff_lambda,
           diff_subln_g, w_in_odd, w_out_odd, swa_sink, peer_wq, peer_keys, peer_u, peer_v, final_g):
    n_batch, seq, d = x.shape
    depth = w_mod.shape[0]
    assert d == D_MODEL and ctx.shape[1] == CTX_LEN and (n_batch * CTX_LEN) == TOK_TILE
    assert seq % TOK_TILE == 0 and seq // ATT_BLOCK >= NA_WIN_BLOCKS
    tiles_per_batch = seq // TOK_TILE
    lat_blocks = seq // ATT_BLOCK
    rows = seq // GRID_W

    xt = jnp.concatenate([ctx.reshape(n_batch * CTX_LEN, d), x.reshape(n_batch * seq, d)], axis=0)
    cvec = jnp.concatenate([c, c_ctx[None, :],
                            jnp.zeros((MOD_ROWS - n_batch - 1, d), F32)], axis=0)
    mod = _modulation(cvec, w_mod, b_mod)
    cos, sin = _rope_tables(n_batch, seq)

    for i in range(depth):
        j = i // 2
        if i % 2 == 0:
            lam_init = 0.8 - 0.6 * math.exp(-0.3 * i)
            qkv = _in_proj(xt, mod[i], norm1_g[i], _even_w_in(w_in_even[j]), cos, sin,
                           (False, False, False, True, True, False), n_batch, tiles_per_batch)
            oa = _na_attention(qkv, _na_table(na_rpb[j], rows), n_batch, lat_blocks)
            ob = _diff_attention(qkv, diff_lambda[j], diff_subln_g[j], lam_init, n_batch, seq)
            w_out = w_out_even[j].astype(BF16)
            o_parts, w_parts = [oa, ob], [w_out[:512], w_out[512:]]
        else:
            qkv = _in_proj(xt, mod[i], norm1_g[i], _odd_w_in(w_in_odd[j]), cos, sin,
                           (True, True, True, False), n_batch, tiles_per_batch)
            o = _swa_attention(qkv, swa_sink[j], n_batch, lat_blocks, seq)
            o_parts, w_parts = [o], [w_out_odd[j].astype(BF16)]
        keys = peer_keys[i].reshape(2 * PEER_HEADS, PEER_NKEYS, PEER_NKEYS).astype(BF16)
        x1, h2, st = _out_proj(o_parts, w_parts, xt, mod[i], norm2_g[i], peer_wq[i].astype(BF16), keys,
                               n_batch, tiles_per_batch)
        tables = _route(st)
        xt = _peer(h2, peer_u[i].astype(BF16), peer_v[i].T.astype(BF16), tables, x1, mod[i],
                   n_batch, tiles_per_batch)

    out = _final_norm(xt, final_g, n_batch * CTX_LEN // TOK_TILE, n_batch * seq // TOK_TILE)
    return out.reshape(n_batch, seq, d)
```

```python
import functools
import math

import jax
import jax.numpy as jnp
import numpy as np
from jax import lax
from jax.experimental import pallas as pl
from jax.experimental.pallas import tpu as pltpu

F32 = jnp.float32
BF16 = jnp.bfloat16

D_MODEL = 1024
CTX_LEN = 256
GRID_W = 64
HEAD_DIM = 64
ROPE_THETA = 10000.0
EPS = 1e-6
NEG_INF = -1e30

NA_HEADS = 8
NA_WIN_H = 8
NA_WIN_W = 16
DIFF_HEADS = 4
SWA_Q_HEADS = 16
SWA_KV_HEADS = 4
SWA_WINDOW = 128
PEER_HEADS = 8
PEER_NKEYS = 128
PEER_TOPK = 16

LANES = 128
TOK_TILE = 512
ATT_BLOCK = 256
NA_ROWS_PER_BLOCK = ATT_BLOCK // GRID_W
NA_WIN_BLOCKS = 3
DIFF_TQ = 512
DIFF_TK = 512
DIFF_PARTS = 4
PEER_STEP = 2048
PEER_CHUNK = 256
PEER_PAIRS_PER_TRIP = 2
PEER_SUB = 8
NOT_TOPK = 999.0
MOD_ROWS = 8
VMEM_LIMIT = 56 * 1024 * 1024


def _cparams(sem):
    return pltpu.CompilerParams(dimension_semantics=sem, vmem_limit_bytes=VMEM_LIMIT)


def _dot(a, b):
    return jnp.dot(a, b, preferred_element_type=F32)


def _dot_nt(a, b):
    return lax.dot_general(a, b, (((1,), (1,)), ((), ())), preferred_element_type=F32)


def _rep_lanes(v, n):
    return v if n == 1 else jnp.concatenate([v] * n, axis=1)


def _mod_row(mod_ref, i, n_batch, tiles_per_batch):
    r = jnp.where(i == 0, n_batch, jnp.maximum(i - 1, 0) // tiles_per_batch)
    return mod_ref[pl.ds(r, 1), :]


def _rmsnorm_rows(x, g):
    ms = jnp.mean(x * x, axis=-1, keepdims=True)
    return x * lax.rsqrt(ms + EPS) * g


def _mod_kernel(cv_ref, w_ref, b_ref, o_ref):
    cv = cv_ref[...]
    s = cv * (1.0 / (1.0 + jnp.exp(-cv)))
    o_ref[0] = jnp.dot(s, w_ref[0], preferred_element_type=F32,
                       precision=lax.Precision.HIGHEST) + b_ref[0]


def _modulation(cvec, w_mod, b_mod):
    depth, d, n = w_mod.shape
    tn = 1536
    return pl.pallas_call(
        _mod_kernel,
        grid=(depth, n // tn),
        in_specs=[
            pl.BlockSpec((MOD_ROWS, d), lambda l, j: (0, 0)),
            pl.BlockSpec((1, d, tn), lambda l, j: (l, 0, j)),
            pl.BlockSpec((1, 1, tn), lambda l, j: (l, 0, j)),
        ],
        out_specs=pl.BlockSpec((1, MOD_ROWS, tn), lambda l, j: (l, 0, j)),
        out_shape=jax.ShapeDtypeStruct((depth, MOD_ROWS, n), F32),
        compiler_params=_cparams(("arbitrary", "arbitrary")),
        name="modulation",
    )(cvec, w_mod, b_mod.reshape(depth, 1, n))


def _in_kernel(x_ref, mod_ref, g_ref, w_ref, cos_ref, sin_ref, o_ref, *, n_batch, tiles_per_batch,
               rope_chunks):
    d = D_MODEL
    mrow = _mod_row(mod_ref, pl.program_id(0), n_batch, tiles_per_batch)
    sh = mrow[:, 0:d]
    sc = mrow[:, d:2 * d]
    h = (_rmsnorm_rows(x_ref[...], g_ref[...]) * (1.0 + sc) + sh).astype(BF16)
    lane = lax.broadcasted_iota(jnp.int32, (1, LANES), 1)
    first = (lane % 32) < 16
    for c, rope in enumerate(rope_chunks):
        cols = slice(c * 512, (c + 1) * 512)
        p = _dot(h, w_ref[:, cols])
        if rope:
            cos = cos_ref[...]
            sin = sin_ref[...]
            parts = []
            for s in range(512 // LANES):
                ps = p[:, s * LANES:(s + 1) * LANES]
                rot = jnp.where(first, pltpu.roll(ps, LANES - 16, 1), pltpu.roll(ps, 16, 1))
                parts.append(ps * cos + rot * sin)
            p = jnp.concatenate(parts, axis=1)
        o_ref[:, cols] = p.astype(BF16)


def _in_proj(x, mod, g, w, cos, sin, rope_chunks, n_batch, tiles_per_batch):
    t, d = x.shape
    n = w.shape[1]
    kern = functools.partial(_in_kernel, n_batch=n_batch, tiles_per_batch=tiles_per_batch,
                             rope_chunks=rope_chunks)
    return pl.pallas_call(
        kern,
        grid=(t // TOK_TILE,),
        in_specs=[
            pl.BlockSpec((TOK_TILE, d), lambda i: (i, 0)),
            pl.BlockSpec(mod.shape, lambda i: (0, 0)),
            pl.BlockSpec((1, d), lambda i: (0, 0)),
            pl.BlockSpec((d, n), lambda i: (0, 0)),
            pl.BlockSpec((TOK_TILE, LANES), lambda i: (i, 0)),
            pl.BlockSpec((TOK_TILE, LANES), lambda i: (i, 0)),
        ],
        out_specs=pl.BlockSpec((TOK_TILE, n), lambda i: (i, 0)),
        out_shape=jax.ShapeDtypeStruct((t, n), BF16),
        compiler_params=_cparams(("arbitrary",)),
        name="in_proj",
    )(x, mod, g.reshape(1, d), w, cos, sin)


def _half_masks():
    lane = lax.broadcasted_iota(jnp.int32, (1, LANES), 1)
    lo = lane < HEAD_DIM
    return lo, jnp.logical_not(lo)


def _softmax_pv(score_parts, v_parts, extra_logit=None):
    m = None
    for s in score_parts:
        mm = jnp.max(s, axis=-1, keepdims=True)
        m = mm if m is None else jnp.maximum(m, mm)
    if extra_logit is not None:
        m = jnp.maximum(m, extra_logit)
    l = None
    o = None
    for s, v in zip(score_parts, v_parts):
        p = jnp.exp(s - m)
        ll = jnp.sum(p, axis=-1, keepdims=True)
        oo = _dot(p.astype(BF16), v)
        l = ll if l is None else l + ll
        o = oo if o is None else o + oo
    if extra_logit is not None:
        l = l + jnp.exp(extra_logit - m)
    return o / l


def _na_kernel(q_ref, k0_ref, k1_ref, k2_ref, v0_ref, v1_ref, v2_ref, kc_ref, vc_ref, tab_ref, o_ref):
    lo, hi = _half_masks()
    k_refs = (k0_ref, k1_ref, k2_ref)
    v_refs = (v0_ref, v1_ref, v2_ref)
    for s in range(NA_HEADS // 2):
        sl = slice(s * LANES, (s + 1) * LANES)
        qs = q_ref[:, sl]
        outs = []
        for half, msk in enumerate((lo, hi)):
            qm = jnp.where(msk, qs, jnp.zeros_like(qs))
            head = 2 * s + half
            scores = [_dot_nt(qm, k_refs[w][:, sl])
                      + tab_ref[0, head, :, w * ATT_BLOCK:(w + 1) * ATT_BLOCK]
                      for w in range(NA_WIN_BLOCKS)]
            scores.append(_dot_nt(qm, kc_ref[:, sl]))
            vals = [v_refs[w][:, sl] for w in range(NA_WIN_BLOCKS)] + [vc_ref[:, sl]]
            outs.append(_softmax_pv(scores, vals))
        o_ref[:, sl] = jnp.where(lo, outs[0], outs[1]).astype(BF16)


def _na_table(rpb, rows):
    win_rows = NA_WIN_BLOCKS * NA_ROWS_PER_BLOCK
    qc = np.arange(GRID_W)
    cs = np.clip(qc - NA_WIN_W // 2, 0, GRID_W - NA_WIN_W)
    kc = np.arange(GRID_W)
    valid_c = (kc[None, :] >= cs[:, None]) & (kc[None, :] < cs[:, None] + NA_WIN_W)
    dc = kc[None, :] - qc[:, None] + (NA_WIN_W - 1)
    sel_c = (valid_c[:, :, None] & (dc[:, :, None] == np.arange(2 * NA_WIN_W - 1))).astype(np.float32)

    def row_geometry(r0, base_row):
        r = r0 + np.arange(NA_ROWS_PER_BLOCK)
        rs = np.clip(r - NA_WIN_H // 2, 0, rows - NA_WIN_H)
        kr = base_row + np.arange(win_rows)
        valid_r = (kr[None, :] >= rs[:, None]) & (kr[None, :] < rs[:, None] + NA_WIN_H)
        dr = kr[None, :] - r[:, None] + (NA_WIN_H - 1)
        sel_r = (valid_r[:, :, None] & (dr[:, :, None] == np.arange(2 * NA_WIN_H - 1))).astype(np.float32)
        return valid_r, sel_r

    geoms = [row_geometry(0, 0), row_geometry(NA_ROWS_PER_BLOCK, 0),
             row_geometry(rows - NA_ROWS_PER_BLOCK, rows - win_rows)]
    sel_r = np.stack([np.zeros_like(geoms[0][1])] + [g[1] for g in geoms])
    valid = np.stack([np.zeros_like(geoms[0][0])] + [g[0] for g in geoms])
    valid = valid[:, :, None, :, None] & valid_c[None, None, :, None, :]
    mask = np.where(valid, 0.0, NEG_INF).astype(np.float32)
    bias = jnp.einsum("hrc,gqkr,xyc->ghqxky", rpb.astype(F32), sel_r, sel_c,
                      precision=lax.Precision.HIGHEST)
    table = bias + mask[:, None]
    return table.reshape(4, rpb.shape[0], ATT_BLOCK, NA_WIN_BLOCKS * ATT_BLOCK)


def _na_attention(qkv, table, n_batch, lat_blocks):
    t = qkv.shape[0]
    nb, lb = n_batch, lat_blocks

    def q_row(b, j):
        return jnp.where(j == 0, b, nb + b * lb + jnp.maximum(j - 1, 0))

    def win_row(w):
        def f(b, j):
            return nb + b * lb + jnp.clip(j - 2, 0, lb - NA_WIN_BLOCKS) + w
        return f

    def case(b, j):
        return jnp.where(j == 0, 0, jnp.where(j == 1, 1, jnp.where(j == lb, 3, 2)))

    blk = (ATT_BLOCK, 512)
    in_specs = [pl.BlockSpec(blk, lambda b, j: (q_row(b, j), 0))]
    in_specs += [pl.BlockSpec(blk, lambda b, j, w=w: (win_row(w)(b, j), 1)) for w in range(NA_WIN_BLOCKS)]
    in_specs += [pl.BlockSpec(blk, lambda b, j, w=w: (win_row(w)(b, j), 2)) for w in range(NA_WIN_BLOCKS)]
    in_specs += [pl.BlockSpec(blk, lambda b, j: (b, 1)), pl.BlockSpec(blk, lambda b, j: (b, 2))]
    in_specs += [pl.BlockSpec((1,) + table.shape[1:], lambda b, j: (case(b, j), 0, 0, 0))]
    return pl.pallas_call(
        _na_kernel,
        grid=(nb, lb + 1),
        in_specs=in_specs,
        out_specs=pl.BlockSpec(blk, lambda b, j: (q_row(b, j), 0)),
        out_shape=jax.ShapeDtypeStruct((t, 512), BF16),
        compiler_params=_cparams(("arbitrary", "arbitrary")),
        name="na_attention",
    )(qkv, qkv, qkv, qkv, qkv, qkv, qkv, qkv, qkv, table)


def _diff_kernel(*refs, latent, lam_init):
    if latent:
        q_ref, kc_ref, vc_ref = refs[:3]
        kv_refs = refs[3:3 + 2 * DIFF_PARTS]
        lam_ref, g_ref, o_ref, m_scr, l_scr, acc_scr = refs[3 + 2 * DIFF_PARTS:]
        j = pl.program_id(1)
        t = pl.program_id(2)
        last = pl.num_programs(2) - 1
    else:
        q_ref, kc_ref, vc_ref, lam_ref, g_ref, _, o_ref, m_scr, l_scr, acc_scr = refs
    lo, hi = _half_masks()

    def process(kv_parts):
        for h in range(DIFF_HEADS):
            sl = slice(h * LANES, (h + 1) * LANES)
            qs = q_ref[:, sl]
            for mp, msk in enumerate((lo, hi)):
                idx = 2 * h + mp
                qm = jnp.where(msk, qs, jnp.zeros_like(qs))
                scores = [_dot_nt(qm, kr[:, sl]) for kr, _ in kv_parts]
                m_prev = m_scr[idx]
                m_new = m_prev
                for s in scores:
                    m_new = jnp.maximum(m_new, jnp.max(s, axis=-1, keepdims=True))
                alpha = jnp.exp(m_prev - m_new)
                l_new = alpha * l_scr[idx]
                acc_new = alpha * acc_scr[idx]
                for s, (_, vr) in zip(scores, kv_parts):
                    p = jnp.exp(s - _rep_lanes(m_new, s.shape[1] // LANES))
                    l_new = l_new + jnp.sum(p, axis=-1, keepdims=True)
                    acc_new = acc_new + _dot(p.astype(BF16), vr[:, sl])
                l_scr[idx] = l_new
                acc_scr[idx] = acc_new
                m_scr[idx] = m_new

    def init():
        m_scr[...] = jnp.full(m_scr.shape, NEG_INF, F32)
        l_scr[...] = jnp.zeros(l_scr.shape, F32)
        acc_scr[...] = jnp.zeros(acc_scr.shape, F32)

    def finalize():
        lp = lam_ref[...]
        lam = (jnp.exp(jnp.sum(lp[0:1] * lp[1:2], axis=-1, keepdims=True))
               - jnp.exp(jnp.sum(lp[2:3] * lp[3:4], axis=-1, keepdims=True)) + lam_init)
        g = g_ref[...]
        for h in range(DIFF_HEADS):
            o = acc_scr[2 * h] / l_scr[2 * h] - lam * (acc_scr[2 * h + 1] / l_scr[2 * h + 1])
            y = _rmsnorm_rows(o, g) * (1.0 - lam_init)
            o_ref[:, h * LANES:(h + 1) * LANES] = y.astype(BF16)

    if latent:
        @pl.when(jnp.logical_and(jnp.logical_and(j == 0, t == 0), pl.program_id(0) == 0))
        def _():
            o_ref[...] = jnp.zeros(o_ref.shape, o_ref.dtype)

        @pl.when(jnp.logical_and(j > 0, t == 0))
        def _():
            init()
            process([(kc_ref, vc_ref)])

        @pl.when(j > 0)
        def _():
            process([(kv_refs[2 * i], kv_refs[2 * i + 1]) for i in range(DIFF_PARTS)])

        @pl.when(jnp.logical_and(j > 0, t == last))
        def _():
            finalize()
    else:
        init()
        process([(kc_ref, vc_ref)])
        finalize()


def _diff_attention(qkv, lam_p, subln_g, lam_init, n_batch, seq):
    t = qkv.shape[0]
    nb = n_batch
    nq = seq // DIFF_TQ
    nk = seq // DIFF_TK
    ctx_tiles_q = nb * CTX_LEN // DIFF_TQ
    ctx_tiles_k = nb * CTX_LEN // DIFF_TK
    scratch = lambda tq: [pltpu.VMEM((2 * DIFF_HEADS, tq, LANES), F32),
                          pltpu.VMEM((2 * DIFF_HEADS, tq, LANES), F32),
                          pltpu.VMEM((2 * DIFF_HEADS, tq, LANES), F32)]
    small = [pl.BlockSpec(lam_p.shape, lambda *a: (0, 0)), pl.BlockSpec((1, LANES), lambda *a: (0, 0))]
    g2 = subln_g.reshape(1, LANES)
    assert ctx_tiles_q == 1 and nk % DIFF_PARTS == 0

    def q_tile(b, j):
        first = ctx_tiles_q + b * nq
        return jnp.where(j == 0, jnp.where(b == 0, 0, first), first + jnp.maximum(j - 1, 0))

    def kv_spec(part, col):
        return pl.BlockSpec((DIFF_TK, 512),
                            lambda b, j, k: (ctx_tiles_k + b * nk + DIFF_PARTS * k + part, col))

    out = pl.pallas_call(
        functools.partial(_diff_kernel, latent=True, lam_init=lam_init),
        grid=(nb, nq + 1, nk // DIFF_PARTS),
        in_specs=[
            pl.BlockSpec((DIFF_TQ, 512), lambda b, j, k: (q_tile(b, j), 3)),
            pl.BlockSpec((CTX_LEN, 512), lambda b, j, k: (b, 4)),
            pl.BlockSpec((CTX_LEN, 512), lambda b, j, k: (b, 5)),
        ] + [kv_spec(part, col) for part in range(DIFF_PARTS) for col in (4, 5)] + small,
        out_specs=pl.BlockSpec((DIFF_TQ, 512), lambda b, j, k: (q_tile(b, j), 0)),
        out_shape=jax.ShapeDtypeStruct((t, 512), BF16),
        scratch_shapes=scratch(DIFF_TQ),
        compiler_params=_cparams(("arbitrary", "arbitrary", "arbitrary")),
        name="diff_attention",
    )(*([qkv] * (3 + 2 * DIFF_PARTS)), lam_p, g2)
    return pl.pallas_call(
        functools.partial(_diff_kernel, latent=False, lam_init=lam_init),
        grid=(nb,),
        in_specs=[
            pl.BlockSpec((CTX_LEN, 512), lambda b: (b, 3)),
            pl.BlockSpec((CTX_LEN, 512), lambda b: (b, 4)),
            pl.BlockSpec((CTX_LEN, 512), lambda b: (b, 5)),
        ] + small + [pl.BlockSpec(memory_space=pl.ANY)],
        out_specs=pl.BlockSpec((CTX_LEN, 512), lambda b: (b, 0)),
        out_shape=jax.ShapeDtypeStruct((t, 512), BF16),
        scratch_shapes=scratch(CTX_LEN),
        input_output_aliases={5: 0},
        compiler_params=_cparams(("arbitrary",)),
        name="diff_attention_ctx",
    )(qkv, qkv, qkv, lam_p, g2, out)


def _swa_kernel(sink_ref, q_ref, k0_ref, k1_ref, k2_ref, v0_ref, v1_ref, v2_ref, kc_ref, vc_ref, o_ref,
                *, seq):
    j = pl.program_id(1)
    jl = j - 1
    lo, hi = _half_masks()
    k_refs = (k0_ref, k1_ref, k2_ref)
    v_refs = (v0_ref, v1_ref, v2_ref)
    qpos = jl * ATT_BLOCK + lax.broadcasted_iota(jnp.int32, (ATT_BLOCK, ATT_BLOCK), 0)
    kiota = lax.broadcasted_iota(jnp.int32, (ATT_BLOCK, ATT_BLOCK), 1)
    valid = []
    for w in range(3):
        kpos = (jl - 1 + w) * ATT_BLOCK + kiota
        valid.append((jnp.abs(kpos - qpos) <= SWA_WINDOW) & (kpos >= 0) & (kpos < seq) & (j >= 1))
    group = SWA_Q_HEADS // SWA_KV_HEADS
    for s in range(SWA_Q_HEADS // 2):
        sl = slice(s * LANES, (s + 1) * LANES)
        n = (2 * s) // group
        kvsl = slice(n * LANES, (n + 1) * LANES)
        qs = q_ref[:, sl]
        outs = []
        for half, msk in enumerate((lo, hi)):
            qm = jnp.where(msk, qs, jnp.zeros_like(qs))
            scores = [jnp.where(valid[w], _dot_nt(qm, k_refs[w][:, kvsl]), NEG_INF) for w in range(3)]
            scores.append(_dot_nt(qm, kc_ref[:, kvsl]))
            vals = [v_refs[w][:, kvsl] for w in range(3)] + [vc_ref[:, kvsl]]
            sink = jnp.full((1, 1), sink_ref[2 * s + half], F32)
            outs.append(_softmax_pv(scores, vals, extra_logit=sink))
        o_ref[:, sl] = jnp.where(lo, outs[0], outs[1]).astype(BF16)


def _swa_attention(qkv, sink, n_batch, lat_blocks, seq):
    t = qkv.shape[0]
    nb, lb = n_batch, lat_blocks

    def q_row(b, j):
        return jnp.where(j == 0, b, nb + b * lb + jnp.maximum(j - 1, 0))

    def win_row(w):
        def f(b, j):
            return nb + b * lb + jnp.clip(j - 2 + w, 0, lb - 1)
        return f

    qblk = (ATT_BLOCK, 1024)
    kblk = (ATT_BLOCK, 512)
    in_specs = [pl.BlockSpec(memory_space=pltpu.SMEM),
                pl.BlockSpec(qblk, lambda b, j: (q_row(b, j), 0))]
    in_specs += [pl.BlockSpec(kblk, lambda b, j, w=w: (win_row(w)(b, j), 2)) for w in range(3)]
    in_specs += [pl.BlockSpec(kblk, lambda b, j, w=w: (win_row(w)(b, j), 3)) for w in range(3)]
    in_specs += [pl.BlockSpec(kblk, lambda b, j: (b, 2)), pl.BlockSpec(kblk, lambda b, j: (b, 3))]
    return pl.pallas_call(
        functools.partial(_swa_kernel, seq=seq),
        grid=(nb, lb + 1),
        in_specs=in_specs,
        out_specs=pl.BlockSpec(qblk, lambda b, j: (q_row(b, j), 0)),
        out_shape=jax.ShapeDtypeStruct((t, 1024), BF16),
        compiler_params=_cparams(("arbitrary", "arbitrary")),
        name="swa_attention",
    )(sink, qkv, qkv, qkv, qkv, qkv, qkv, qkv, qkv, qkv)


def _out_kernel(*refs, n_parts, n_batch, tiles_per_batch):
    o_refs = refs[:n_parts]
    w_refs = refs[n_parts:2 * n_parts]
    x_ref, mod_ref, g_ref, wq_ref, keys_ref, x1_ref, h2_ref, st_ref = refs[2 * n_parts:]
    d = D_MODEL
    y = None
    for o_ref, w_ref in zip(o_refs, w_refs):
        yy = _dot(o_ref[...], w_ref[...])
        y = yy if y is None else y + yy
    mrow = _mod_row(mod_ref, pl.program_id(0), n_batch, tiles_per_batch)
    g1 = mrow[:, 2 * d:3 * d]
    sh2 = mrow[:, 3 * d:4 * d]
    sc2 = mrow[:, 4 * d:5 * d]
    x1 = x_ref[...] + g1 * y
    x1_ref[...] = x1
    h = (_rmsnorm_rows(x1, g_ref[...]) * (1.0 + sc2) + sh2).astype(BF16)
    h2_ref[...] = h
    for hp in range(2 * PEER_HEADS):
        qh = _dot(h, wq_ref[:, hp * LANES:(hp + 1) * LANES]).astype(BF16)
        st_ref[0, hp] = _dot_nt(keys_ref[hp], qh)


def _out_proj(o_parts, w_parts, x, mod, g, wq, keys, n_batch, tiles_per_batch):
    t, d = x.shape
    nt = t // TOK_TILE
    n_parts = len(o_parts)
    in_specs = [pl.BlockSpec((TOK_TILE, o.shape[1]), lambda i: (i, 0)) for o in o_parts]
    in_specs += [pl.BlockSpec(w.shape, lambda i: (0, 0)) for w in w_parts]
    in_specs += [
        pl.BlockSpec((TOK_TILE, d), lambda i: (i, 0)),
        pl.BlockSpec(mod.shape, lambda i: (0, 0)),
        pl.BlockSpec((1, d), lambda i: (0, 0)),
        pl.BlockSpec(wq.shape, lambda i: (0, 0)),
        pl.BlockSpec(keys.shape, lambda i: (0, 0, 0)),
    ]
    kern = functools.partial(_out_kernel, n_parts=n_parts, n_batch=n_batch, tiles_per_batch=tiles_per_batch)
    return pl.pallas_call(
        kern,
        grid=(nt,),
        in_specs=in_specs,
        out_specs=[
            pl.BlockSpec((TOK_TILE, d), lambda i: (i, 0)),
            pl.BlockSpec((TOK_TILE, d), lambda i: (i, 0)),
            pl.BlockSpec((1, 2 * PEER_HEADS, PEER_NKEYS, TOK_TILE), lambda i: (i, 0, 0, 0)),
        ],
        out_shape=[
            jax.ShapeDtypeStruct((t, d), F32),
            jax.ShapeDtypeStruct((t, d), BF16),
            jax.ShapeDtypeStruct((nt, 2 * PEER_HEADS, PEER_NKEYS, TOK_TILE), F32),
        ],
        compiler_params=_cparams(("arbitrary",)),
        name="out_proj",
    )(*o_parts, *w_parts, x, mod, g.reshape(1, d), wq, keys)


def _rank_topk(a, sorted_ref):
    iota = lax.broadcasted_iota(jnp.int32, a.shape, 0)

    def body(r, carry):
        cur, rank = carry
        m = jnp.max(cur, axis=0, keepdims=True)
        first = jnp.min(jnp.where(cur == m, iota, PEER_NKEYS), axis=0, keepdims=True)
        sel = iota == first
        sorted_ref[pl.ds(r, 1), :] = m
        return jnp.where(sel, -jnp.inf, cur), jnp.where(sel, jnp.asarray(r, F32), rank)

    _, rank = lax.fori_loop(0, PEER_TOPK, body, (a, jnp.full(a.shape, NOT_TOPK, F32)))
    return rank


def _sorted_pair_distinct(a_ref, b_ref, sa_ref, sb_ref):
    shape = (1, a_ref.shape[1])

    def body(r, carry):
        pa, pb = carry
        a = a_ref[...]
        b = b_ref[...]
        ma = jnp.max(jnp.where(a < pa, a, -jnp.inf), axis=0, keepdims=True)
        mb = jnp.max(jnp.where(b < pb, b, -jnp.inf), axis=0, keepdims=True)
        sa_ref[pl.ds(r, 1), :] = ma
        sb_ref[pl.ds(r, 1), :] = mb
        return ma, mb

    inf = jnp.full(shape, jnp.inf, F32)
    last_a, last_b = lax.fori_loop(0, PEER_TOPK, body, (inf, inf))
    cnt_a = jnp.sum(jnp.where(a_ref[...] >= last_a, 1.0, 0.0), axis=0, keepdims=True)
    cnt_b = jnp.sum(jnp.where(b_ref[...] >= last_b, 1.0, 0.0), axis=0, keepdims=True)
    return cnt_a, cnt_b


def _route_kernel(st_ref, r2_ref, eb_ref, nn_ref, ea_ref, sa_ref, sb_ref, ra_ref, rb_ref, bc_ref):
    k = PEER_TOPK
    kf = float(k)
    a_ref = st_ref.at[0, 0]
    b_ref = st_ref.at[0, 1]
    cnt_a, cnt_b = _sorted_pair_distinct(a_ref, b_ref, sa_ref, sb_ref)
    ties = jnp.max(jnp.maximum(jnp.abs(cnt_a - kf), jnp.abs(cnt_b - kf))) > 0.0

    @pl.when(ties)
    def _():
        ra_ref[...] = _rank_topk(a_ref[...], sa_ref)
        rb_ref[...] = _rank_topk(b_ref[...], sb_ref)

    sa = sa_ref[...]
    sb = sb_ref[...]
    riota = lax.broadcasted_iota(jnp.int32, sa.shape, 0)

    def walk(_, n):
        sbn = jnp.full(sa.shape, -jnp.inf, F32)
        for r2 in range(k):
            sbn = jnp.where(n == r2, sb[r2:r2 + 1, :], sbn)
        f = sa + sbn
        m = jnp.max(f, axis=0, keepdims=True)
        first = jnp.min(jnp.where(f == m, riota, k), axis=0, keepdims=True)
        return n + (riota == first).astype(jnp.int32)

    n = lax.fori_loop(0, k, walk, jnp.zeros(sa.shape, jnp.int32))
    ea_s = jnp.exp(sa - sa[0:1, :])
    eb_s = jnp.exp(sb - sb[0:1, :])
    z = jnp.zeros((1, sa.shape[1]), F32)
    for r2 in range(k):
        z = z + eb_s[r2:r2 + 1, :] * jnp.sum(jnp.where(n > r2, ea_s, 0.0), axis=0, keepdims=True)
    nf = n.astype(F32)

    @pl.when(ties)
    def _():
        a = a_ref[...]
        ra = ra_ref[...]
        nn = jnp.zeros(a.shape, F32)
        for r1 in range(k):
            nn = jnp.where(ra == float(r1), nf[r1:r1 + 1, :], nn)
        r2_ref[0, 0] = rb_ref[...]
        eb_ref[0, 0] = jnp.exp(b_ref[...] - sb[0:1, :])
        nn_ref[0, 0] = nn
        ea_ref[0, 0] = jnp.where(ra < kf, jnp.exp(a - sa[0:1, :]), 0.0) / z

    @pl.when(jnp.logical_not(ties))
    def _():
        rows_per = bc_ref.shape[2]
        for r in range(k):
            bc_ref[0, r] = jnp.broadcast_to(sb[r:r + 1, :], bc_ref.shape[2:])
            bc_ref[1, r] = jnp.broadcast_to(sa[r:r + 1, :], bc_ref.shape[2:])
            bc_ref[2, r] = jnp.broadcast_to(nf[r:r + 1, :], bc_ref.shape[2:])
        bc_ref[3, 0] = jnp.broadcast_to(1.0 / z, bc_ref.shape[2:])

        def block(kb, carry):
            rows = pl.ds(pl.multiple_of(kb * rows_per, rows_per), rows_per)
            aa = a_ref[rows, :]
            bb = b_ref[rows, :]
            rank = jnp.full(bb.shape, kf, F32)
            nn = jnp.zeros(aa.shape, F32)
            for r in reversed(range(k)):
                rank = jnp.where(bc_ref[0, r] <= bb, float(r), rank)
                nn = jnp.where(bc_ref[1, r] == aa, bc_ref[2, r], nn)
            r2_ref[0, 0, rows, :] = rank
            eb_ref[0, 0, rows, :] = jnp.exp(bb - bc_ref[0, 0])
            nn_ref[0, 0, rows, :] = nn
            ea_ref[0, 0, rows, :] = jnp.where(aa >= bc_ref[1, k - 1], jnp.exp(aa - bc_ref[1, 0]),
                                              0.0) * bc_ref[3, 0]
            return carry

        lax.fori_loop(0, PEER_NKEYS // rows_per, block, 0)


def _route(st):
    nt = st.shape[0]
    tab = jax.ShapeDtypeStruct((nt, PEER_HEADS, PEER_NKEYS, TOK_TILE), F32)
    spec = pl.BlockSpec((1, 1, PEER_NKEYS, TOK_TILE), lambda i, h: (i, h, 0, 0))
    return pl.pallas_call(
        _route_kernel,
        grid=(nt, PEER_HEADS),
        in_specs=[pl.BlockSpec((1, 2, PEER_NKEYS, TOK_TILE), lambda i, h: (i, h, 0, 0))],
        out_specs=[spec, spec, spec, spec],
        out_shape=[tab, tab, tab, tab],
        scratch_shapes=[pltpu.VMEM((PEER_TOPK, TOK_TILE), F32), pltpu.VMEM((PEER_TOPK, TOK_TILE), F32),
                        pltpu.VMEM((PEER_NKEYS, TOK_TILE), F32), pltpu.VMEM((PEER_NKEYS, TOK_TILE), F32),
                        pltpu.VMEM((4, PEER_TOPK, 8, TOK_TILE), F32)],
        compiler_params=_cparams(("arbitrary", "arbitrary")),
        name="peer_route",
    )(st)


def _peer_kernel(h_ref, u_ref, vtp_ref, r2_ref, eb_ref, nn_ref, ea_ref, x_ref, mod_ref,
                 o_ref, acc_ref, st_ref, ga_ref, *, n_steps, n_batch, tiles_per_batch):
    j = pl.program_id(1)
    d = D_MODEL
    pairs_per_half = PEER_PAIRS_PER_TRIP
    halves = PEER_STEP // PEER_CHUNK // pairs_per_half
    rows_per_half = d // halves
    i1_per_pair = PEER_CHUNK // PEER_NKEYS

    @pl.when(j == 0)
    def _():
        acc_ref[...] = jnp.zeros(acc_ref.shape, F32)
        ga_ref[...] = jnp.zeros(ga_ref.shape, BF16)

    def second_matmul(read_slot, it):
        if isinstance(it, int):
            rows = slice(it * rows_per_half, (it + 1) * rows_per_half)
        else:
            rows = pl.ds(pl.multiple_of(it * rows_per_half, rows_per_half), rows_per_half)
        acc_ref[rows, :] += _dot(vtp_ref[rows, :], ga_ref[read_slot])

    def step(write_slot, read_slot):
        def body(it, carry):
            second_matmul(read_slot, it)
            starts = []
            for p in range(pairs_per_half):
                e0 = pl.multiple_of((it * pairs_per_half + p) * PEER_CHUNK, PEER_CHUNK)
                starts.append(e0)
                st_ref[p] = _dot_nt(u_ref[pl.ds(e0, PEER_CHUNK), :], h_ref[...])
            for p in range(pairs_per_half):
                for q in range(i1_per_pair):
                    i1 = (j * (PEER_STEP // PEER_NKEYS) + (it * pairs_per_half + p) * i1_per_pair + q)
                    n_rows = [nn_ref[0, hd, pl.ds(i1, 1), :] for hd in range(PEER_HEADS)]
                    ea_rows = [ea_ref[0, hd, pl.ds(i1, 1), :] for hd in range(PEER_HEADS)]
                    for lc in range(TOK_TILE // LANES):
                        lanes = slice(lc * LANES, (lc + 1) * LANES)
                        n_blocks = PEER_NKEYS // PEER_SUB
                        gates = [None] * n_blocks
                        for hd in range(PEER_HEADS):
                            n_b = jnp.broadcast_to(n_rows[hd][:, lanes], (PEER_SUB, LANES))
                            ea_b = jnp.broadcast_to(ea_rows[hd][:, lanes], (PEER_SUB, LANES))
                            for rb in range(n_blocks):
                                keys = slice(rb * PEER_SUB, (rb + 1) * PEER_SUB)
                                term = jnp.where(r2_ref[0, hd, keys, lanes] < n_b,
                                                 ea_b * eb_ref[0, hd, keys, lanes], 0.0)
                                gates[rb] = term if gates[rb] is None else gates[rb] + term
                        for rb in range(n_blocks // 2):
                            blocks = []
                            for s in range(2):
                                k0 = (2 * rb + s) * PEER_SUB
                                st = st_ref[p, q * PEER_NKEYS + k0:q * PEER_NKEYS + k0 + PEER_SUB, lanes]
                                act = 0.5 * st * (1.0 + lax.erf(st * (1.0 / math.sqrt(2.0))))
                                blocks.append(gates[2 * rb + s] * act)
                            r0 = starts[p] + q * PEER_NKEYS + 2 * rb * PEER_SUB
                            ga_ref[write_slot, pl.ds(pl.multiple_of(r0, 2 * PEER_SUB), 2 * PEER_SUB),
                                   lanes] = jnp.concatenate(blocks, axis=0).astype(BF16)
            return carry

        lax.fori_loop(0, halves, body, 0)

    for slot in range(2):
        @pl.when(jnp.logical_and(j < n_steps, j % 2 == slot))
        def _(slot=slot):
            step(slot, 1 - slot)

    @pl.when(j == n_steps)
    def _():
        for it in range(halves):
            second_matmul((n_steps - 1) % 2, it)
        mrow = _mod_row(mod_ref, pl.program_id(0), n_batch, tiles_per_batch)
        g2 = mrow[:, 5 * d:6 * d]
        o_ref[...] = x_ref[...] + g2 * acc_ref[...].T


def _peer(h2, u, vt, tables, x1, mod, n_batch, tiles_per_batch):
    t, d = x1.shape
    n = u.shape[0] // PEER_STEP
    pairs_per_half = PEER_PAIRS_PER_TRIP
    tab_spec = pl.BlockSpec((1, PEER_HEADS, PEER_NKEYS, TOK_TILE), lambda i, j: (i, 0, 0, 0))
    kern = functools.partial(_peer_kernel, n_steps=n, n_batch=n_batch, tiles_per_batch=tiles_per_batch)
    return pl.pallas_call(
        kern,
        grid=(t // TOK_TILE, n + 1),
        in_specs=[
            pl.BlockSpec((TOK_TILE, d), lambda i, j: (i, 0)),
            pl.BlockSpec((PEER_STEP, d), lambda i, j: (jnp.minimum(j, n - 1), 0)),
            pl.BlockSpec((d, PEER_STEP), lambda i, j: (0, jnp.maximum(j - 1, 0))),
            tab_spec, tab_spec, tab_spec, tab_spec,
            pl.BlockSpec((TOK_TILE, d), lambda i, j: (i, 0)),
            pl.BlockSpec(mod.shape, lambda i, j: (0, 0)),
        ],
        out_specs=pl.BlockSpec((TOK_TILE, d), lambda i, j: (i, 0)),
        out_shape=jax.ShapeDtypeStruct((t, d), F32),
        scratch_shapes=[pltpu.VMEM((d, TOK_TILE), F32),
                        pltpu.VMEM((pairs_per_half, PEER_CHUNK, TOK_TILE), F32),
                        pltpu.VMEM((2, PEER_STEP, TOK_TILE), BF16)],
        compiler_params=_cparams(("arbitrary", "arbitrary")),
        name="peer_experts",
    )(h2, u, vt, *tables, x1, mod)


def _final_kernel(x_ref, g_ref, o_ref):
    o_ref[...] = _rmsnorm_rows(x_ref[...], g_ref[...])


def _final_norm(x, g, first_tile, n_tiles):
    d = x.shape[1]
    return pl.pallas_call(
        _final_kernel,
        grid=(n_tiles,),
        in_specs=[pl.BlockSpec((TOK_TILE, d), lambda i: (first_tile + i, 0)),
                  pl.BlockSpec((1, d), lambda i: (0, 0))],
        out_specs=pl.BlockSpec((TOK_TILE, d), lambda i: (i, 0)),
        out_shape=jax.ShapeDtypeStruct((n_tiles * TOK_TILE, d), F32),
        compiler_params=_cparams(("arbitrary",)),
        name="final_norm",
    )(x, g.reshape(1, d))


def _rope_tables(n_batch, seq):
    pos = jnp.arange(seq)
    row = (pos // GRID_W).astype(F32)
    col = (pos % GRID_W).astype(F32)
    half = HEAD_DIM // 2
    inv = ROPE_THETA ** (-jnp.arange(0, half, 2, dtype=F32) / half)
    ar = row[:, None] * inv
    ac = col[:, None] * inv
    ang = jnp.concatenate([ar, ar, ac, ac], axis=-1)
    sign = jnp.where((jnp.arange(HEAD_DIM) % 32) < 16, -1.0, 1.0).astype(F32)
    cos = jnp.tile(jnp.cos(ang), (n_batch, LANES // HEAD_DIM))
    sin = jnp.tile(jnp.sin(ang) * sign, (n_batch, LANES // HEAD_DIM))
    n_ctx = n_batch * CTX_LEN
    cos = jnp.concatenate([jnp.ones((n_ctx, LANES), F32), cos], axis=0)
    sin = jnp.concatenate([jnp.zeros((n_ctx, LANES), F32), sin], axis=0)
    return cos, sin


def _even_w_in(w):
    a = NA_HEADS * HEAD_DIM
    scale = jnp.ones((w.shape[1],), F32)
    scale = scale.at[0:a].set(HEAD_DIM ** -0.5)
    scale = scale.at[3 * a:3 * a + 512].set(HEAD_DIM ** -0.5)
    return (w * scale).astype(BF16)


def _odd_w_in(w):
    nq = SWA_Q_HEADS * HEAD_DIM
    nkv = SWA_KV_HEADS * HEAD_DIM
    d = w.shape[0]
    q = w[:, :nq] * (HEAD_DIM ** -0.5)
    k = w[:, nq:nq + nkv].reshape(d, SWA_KV_HEADS, 1, HEAD_DIM)
    v = w[:, nq + nkv:].reshape(d, SWA_KV_HEADS, 1, HEAD_DIM)
    k2 = jnp.broadcast_to(k, (d, SWA_KV_HEADS, 2, HEAD_DIM)).reshape(d, 2 * nkv)
    v2 = jnp.broadcast_to(v, (d, SWA_KV_HEADS, 2, HEAD_DIM)).reshape(d, 2 * nkv)
    return jnp.concatenate([q, k2, v2], axis=1).astype(BF16)


def kernel(x, c, ctx, c_ctx, w_mod, b_mod, norm1_g, norm2_g, w_in_even, w_out_even, na_rpb, diff_lambda,
           diff_subln_g, w_in_odd, w_out_odd, swa_sink, peer_wq, peer_keys, peer_u, peer_v, final_g):
    n_batch, seq, d = x.shape
    depth = w_mod.shape[0]
    assert d == D_MODEL and ctx.shape[1] == CTX_LEN and (n_batch * CTX_LEN) == TOK_TILE
    assert seq % TOK_TILE == 0 and seq // ATT_BLOCK >= NA_WIN_BLOCKS
    tiles_per_batch = seq // TOK_TILE
    lat_blocks = seq // ATT_BLOCK
    rows = seq // GRID_W

    xt = jnp.concatenate([ctx.reshape(n_batch * CTX_LEN, d), x.reshape(n_batch * seq, d)], axis=0)
    cvec = jnp.concatenate([c, c_ctx[None, :],
                            jnp.zeros((MOD_ROWS - n_batch - 1, d), F32)], axis=0)
    mod = _modulation(cvec, w_mod, b_mod)
    cos, sin = _rope_tables(n_batch, seq)

    for i in range(depth):
        j = i // 2
        if i % 2 == 0:
            lam_init = 0.8 - 0.6 * math.exp(-0.3 * i)
            qkv = _in_proj(xt, mod[i], norm1_g[i], _even_w_in(w_in_even[j]), cos, sin,
                           (False, False, False, True, True, False), n_batch, tiles_per_batch)
            oa = _na_attention(qkv, _na_table(na_rpb[j], rows), n_batch, lat_blocks)
            ob = _diff_attention(qkv, diff_lambda[j], diff_subln_g[j], lam_init, n_batch, seq)
            w_out = w_out_even[j].astype(BF16)
            o_parts, w_parts = [oa, ob], [w_out[:512], w_out[512:]]
        else:
            qkv = _in_proj(xt, mod[i], norm1_g[i], _odd_w_in(w_in_odd[j]), cos, sin,
                           (True, True, True, False), n_batch, tiles_per_batch)
            o = _swa_attention(qkv, swa_sink[j], n_batch, lat_blocks, seq)
            o_parts, w_parts = [o], [w_out_odd[j].astype(BF16)]
        keys = peer_keys[i].reshape(2 * PEER_HEADS, PEER_NKEYS, PEER_NKEYS).astype(BF16)
        x1, h2, st = _out_proj(o_parts, w_parts, xt, mod[i], norm2_g[i], peer_wq[i].astype(BF16), keys,
                               n_batch, tiles_per_batch)
        tables = _route(st)
        xt = _peer(h2, peer_u[i].astype(BF16), peer_v[i].T.astype(BF16), tables, x1, mod[i],
                   n_batch, tiles_per_batch)

    out = _final_norm(xt, final_g, n_batch * CTX_LEN // TOK_TILE, n_batch * seq // TOK_TILE)
    return out.reshape(n_batch, seq, d)
```
